```python
import jax, jax.numpy as jnp
from jax import lax
import numpy as np

D_MODEL = 1024
BATCH = 1
SEQ = 16384
DEPTH = 4

GRID_W = 64
CTX_LEN = 256
EPS = 1e-6

MLA_HEADS = 8
MLA_NOPE = 64
MLA_ROPE = 32
MLA_V = 64
MLA_Q_LORA = 768
MLA_KV_LORA = 256
MLA_WIDTH = MLA_HEADS * MLA_V
ROPE_BASE = 10000.0
Q_BLOCK = 128

NA_HEADS = 8
NA_HEAD_DIM = 64
NA_WIDTH = NA_HEADS * NA_HEAD_DIM
NA_KH_MAX = 8
NA_KW = 16

SGU_CHUNK = 128
SGU_WIDTH = 2048
SGU_GROUPS = 8
SGU_GROUP_DIM = SGU_WIDTH // SGU_GROUPS

N_ATT = (DEPTH + 1) // 2
N_SGU = DEPTH // 2

ATT_SPLITS = [MLA_Q_LORA, MLA_KV_LORA, MLA_ROPE, MLA_WIDTH, NA_WIDTH, NA_WIDTH, NA_WIDTH, NA_WIDTH]
ATT_IN = sum(ATT_SPLITS)
ATT_MIX = MLA_WIDTH + NA_WIDTH

kernel_name = "hybrid_mla_natten_sgu_prefix_dit"


def rmsnorm(x, g):
    xf = x.astype(jnp.float32)
    y = xf * lax.rsqrt(jnp.mean(xf * xf, axis=-1, keepdims=True) + EPS)
    return (y * g.astype(jnp.float32)).astype(x.dtype)


def layernorm(x, g, b):
    xf = x.astype(jnp.float32)
    mu = jnp.mean(xf, axis=-1, keepdims=True)
    var = jnp.mean(jnp.square(xf - mu), axis=-1, keepdims=True)
    y = (xf - mu) * lax.rsqrt(var + EPS)
    return (y * g.astype(jnp.float32) + b.astype(jnp.float32)).astype(x.dtype)


def adaln(cond, w_mod, b_mod):
    m = jax.nn.silu(cond) @ w_mod + b_mod
    return jnp.split(m, 3, axis=-1)


def axial_rope(n):
    t = jnp.arange(n)
    row = (t // GRID_W).astype(jnp.float32)
    col = (t % GRID_W).astype(jnp.float32)
    axis_dims = MLA_ROPE // 2
    inv = jnp.power(ROPE_BASE, -jnp.arange(0, axis_dims, 2, dtype=jnp.float32) / axis_dims)
    ang = jnp.concatenate([row[:, None] * inv, col[:, None] * inv], axis=-1)
    return jnp.cos(ang), jnp.sin(ang)


def rotate(x, cos, sin):
    x1, x2 = jnp.split(x, 2, axis=-1)
    cos = cos.astype(x.dtype)
    sin = sin.astype(x.dtype)
    return jnp.concatenate([x1 * cos - x2 * sin, x1 * sin + x2 * cos], axis=-1)


def full_attn(q, k, v):
    s = jnp.einsum('bqhd,bkhd->bhqk', q, k).astype(jnp.float32) * (q.shape[-1] ** -0.5)
    p = jax.nn.softmax(s, axis=-1).astype(v.dtype)
    return jnp.einsum('bhqk,bkhd->bqhd', p, v)


def blocked_attn(q, k, v):
    B, S, H, d = q.shape
    nb = S // Q_BLOCK
    qb = q.reshape(B, nb, Q_BLOCK, H, d).transpose(1, 0, 2, 3, 4)
    scale = d ** -0.5

    def one(qi):
        s = jnp.einsum('bqhd,bkhd->bhqk', qi, k).astype(jnp.float32) * scale
        p = jax.nn.softmax(s, axis=-1).astype(v.dtype)
        return jnp.einsum('bhqk,bkhd->bqhd', p, v)

    o = lax.map(one, qb)
    return o.transpose(1, 0, 2, 3, 4).reshape(B, S, H, v.shape[-1])


def mla_project(cq, ckv, kr, q_norm, w_uq, kv_norm, w_ukv, rope):
    B, n = cq.shape[:2]
    q = (rmsnorm(cq, q_norm) @ w_uq).reshape(B, n, MLA_HEADS, MLA_NOPE + MLA_ROPE)
    kv = (rmsnorm(ckv, kv_norm) @ w_ukv).reshape(B, n, MLA_HEADS, MLA_NOPE + MLA_V)
    q_nope, q_rope = q[..., :MLA_NOPE], q[..., MLA_NOPE:]
    k_nope, v = kv[..., :MLA_NOPE], kv[..., MLA_NOPE:]
    if rope is not None:
        cos, sin = rope
        q_rope = rotate(q_rope, cos[None, :, None, :], sin[None, :, None, :])
        kr = rotate(kr, cos[None], sin[None])
    k_rope = jnp.broadcast_to(kr[:, :, None, :], (B, n, MLA_HEADS, MLA_ROPE))
    return (jnp.concatenate([q_nope, q_rope], axis=-1),
            jnp.concatenate([k_nope, k_rope], axis=-1), v)


def neighbourhood_attn(q, k, v, kc, vc, rpb):
    B, S, H, dh = q.shape
    rows = S // GRID_W
    kh = min(NA_KH_MAX, rows)
    nwin = kh * NA_KW
    qg = q.reshape(B, rows, GRID_W, H, dh)
    kg = k.reshape(B, rows, GRID_W, H, dh)
    vg = v.reshape(B, rows, GRID_W, H, dh)
    qcol = jnp.arange(GRID_W)
    col_start = jnp.clip(qcol - NA_KW // 2, 0, GRID_W - NA_KW)
    col_idx = col_start[:, None] + jnp.arange(NA_KW)[None, :]
    col_bias_idx = col_idx - qcol[:, None] + (NA_KW - 1)
    scale = dh ** -0.5

    def one_row(r):
        r0 = jnp.clip(r - kh // 2, 0, rows - kh)
        qr = lax.dynamic_index_in_dim(qg, r, axis=1, keepdims=False)
        kr = lax.dynamic_slice_in_dim(kg, r0, kh, axis=1)
        vr = lax.dynamic_slice_in_dim(vg, r0, kh, axis=1)
        kw = kr[:, :, col_idx]
        vw = vr[:, :, col_idx]
        row_bias_idx = r0 + jnp.arange(kh) - r + (NA_KH_MAX - 1)
        bias = rpb[:, row_bias_idx[None, :, None], col_bias_idx[:, None, :]]
        s_win = jnp.einsum('bqhd,brqwhd->bhqrw', qr, kw).astype(jnp.float32) * scale + bias[None]
        s_ctx = jnp.einsum('bqhd,bkhd->bhqk', qr, kc).astype(jnp.float32) * scale
        s = jnp.concatenate([s_win.reshape(B, H, GRID_W, nwin), s_ctx], axis=-1)
        p = jax.nn.softmax(s, axis=-1).astype(v.dtype)
        p_win = p[..., :nwin].reshape(B, H, GRID_W, kh, NA_KW)
        p_ctx = p[..., nwin:]
        return (jnp.einsum('bhqrw,brqwhd->bqhd', p_win, vw)
                + jnp.einsum('bhqk,bkhd->bqhd', p_ctx, vc))

    o = lax.map(one_row, jnp.arange(rows))
    return o.transpose(1, 0, 2, 3, 4).reshape(B, S, H, dh)


def attention_mixer(hx, hc, w_in, q_norm, w_uq, kv_norm, w_ukv, rpb, w_out, rope, with_ctx_out):
    B, S, _ = hx.shape
    L = hc.shape[1]
    cq_x, ckv_x, kr_x, gm_x, qn_x, kn_x, vn_x, gn_x = jnp.split(hx @ w_in, np.cumsum(ATT_SPLITS)[:-1].tolist(), axis=-1)
    cq_c, ckv_c, kr_c, gm_c, qn_c, kn_c, vn_c, gn_c = jnp.split(hc @ w_in, np.cumsum(ATT_SPLITS)[:-1].tolist(), axis=-1)
    qx, kx, vx = mla_project(cq_x, ckv_x, kr_x, q_norm, w_uq, kv_norm, w_ukv, rope)
    qc, kc, vc = mla_project(cq_c, ckv_c, kr_c, q_norm, w_uq, kv_norm, w_ukv, None)
    mla_x = blocked_attn(qx, jnp.concatenate([kx, kc], axis=1), jnp.concatenate([vx, vc], axis=1))
    heads = lambda t, n: t.reshape(B, n, NA_HEADS, NA_HEAD_DIM)
    knc, vnc = heads(kn_c, L), heads(vn_c, L)
    na_x = neighbourhood_attn(heads(qn_x, S), heads(kn_x, S), heads(vn_x, S), knc, vnc, rpb)
    yx = jnp.concatenate([mla_x.reshape(B, S, MLA_WIDTH) * jax.nn.silu(gm_x),
                          na_x.reshape(B, S, NA_WIDTH) * jax.nn.silu(gn_x)], axis=-1) @ w_out
    if not with_ctx_out:
        return yx, None
    mla_c = full_attn(qc, kc, vc)
    na_c = full_attn(heads(qn_c, L), knc, vnc)
    yc = jnp.concatenate([mla_c.reshape(B, L, MLA_WIDTH) * jax.nn.silu(gm_c),
                          na_c.reshape(B, L, NA_WIDTH) * jax.nn.silu(gn_c)], axis=-1) @ w_out
    return yx, yc


def sgu_mixer(h, w_in, ln_g, ln_b, w_s, b_s, w_out):
    B, n, _ = h.shape
    u, v, g = jnp.split(h @ w_in, 3, axis=-1)
    u = jax.nn.gelu(u, approximate=False)
    v = layernorm(jax.nn.gelu(v, approximate=False), ln_g, ln_b)
    vg = v.reshape(B, n // SGU_CHUNK, SGU_CHUNK, SGU_GROUPS, SGU_GROUP_DIM)
    sv = jnp.einsum('gts,bcsgd->bctgd', w_s, vg) + b_s.T[None, None, :, :, None]
    return (u * sv.reshape(B, n, SGU_WIDTH) * jax.nn.silu(g)) @ w_out


def setup_inputs(seed: int = 0) -> dict:
    key = jax.random.key(seed)
    ks = jax.random.split(key, 21)
    D = D_MODEL

    def nrm(k, shape, s):
        return jax.random.normal(k, shape, jnp.float32) * s

    return {
        "x": nrm(ks[0], (BATCH, SEQ, D), 1.0),
        "c": nrm(ks[1], (BATCH, D), 1.0),
        "ctx": nrm(ks[2], (BATCH, CTX_LEN, D), 1.0),
        "c_ctx": nrm(ks[3], (D,), 1.0),
        "norm_g": 1.0 + nrm(ks[4], (DEPTH, D), 0.02),
        "w_mod": nrm(ks[5], (DEPTH, D, 3 * D), 0.5 * D ** -0.5),
        "b_mod": nrm(ks[6], (DEPTH, 3 * D), 0.02),
        "att_w_in": nrm(ks[7], (N_ATT, D, ATT_IN), D ** -0.5),
        "mla_q_norm": 1.0 + nrm(ks[8], (N_ATT, MLA_Q_LORA), 0.02),
        "mla_w_uq": nrm(ks[9], (N_ATT, MLA_Q_LORA, MLA_HEADS * (MLA_NOPE + MLA_ROPE)), MLA_Q_LORA ** -0.5),
        "mla_kv_norm": 1.0 + nrm(ks[10], (N_ATT, MLA_KV_LORA), 0.02),
        "mla_w_ukv": nrm(ks[11], (N_ATT, MLA_KV_LORA, MLA_HEADS * (MLA_NOPE + MLA_V)), MLA_KV_LORA ** -0.5),
        "na_rpb": nrm(ks[12], (N_ATT, NA_HEADS, 2 * NA_KH_MAX - 1, 2 * NA_KW - 1), 0.1),
        "att_w_out": nrm(ks[13], (N_ATT, ATT_MIX, D), ATT_MIX ** -0.5),
        "sgu_w_in": nrm(ks[14], (N_SGU, D, 3 * SGU_WIDTH), D ** -0.5),
        "sgu_ln_g": 1.0 + nrm(ks[15], (N_SGU, SGU_WIDTH), 0.02),
        "sgu_ln_b": nrm(ks[16], (N_SGU, SGU_WIDTH), 0.02),
        "sgu_w_s": nrm(ks[17], (N_SGU, SGU_GROUPS, SGU_CHUNK, SGU_CHUNK), 0.5 * SGU_CHUNK ** -0.5),
        "sgu_b_s": 1.0 + nrm(ks[18], (N_SGU, SGU_GROUPS, SGU_CHUNK), 0.02),
        "sgu_w_out": nrm(ks[19], (N_SGU, SGU_WIDTH, D), SGU_WIDTH ** -0.5),
        "final_norm": 1.0 + nrm(ks[20], (D,), 0.02),
    }


def reference(x, c, ctx, c_ctx, norm_g, w_mod, b_mod, att_w_in, mla_q_norm, mla_w_uq, mla_kv_norm,
              mla_w_ukv, na_rpb, att_w_out, sgu_w_in, sgu_ln_g, sgu_ln_b, sgu_w_s, sgu_b_s, sgu_w_out,
              final_norm):
    S = x.shape[1]
    rope = axial_rope(S)
    xc = ctx
    last_ctx_reader = max(l for l in range(DEPTH) if l % 2 == 0)
    for l in range(DEPTH):
        shift_x, scale_x, gate_x = adaln(c, w_mod[l], b_mod[l])
        shift_c, scale_c, gate_c = adaln(c_ctx, w_mod[l], b_mod[l])
        hx = rmsnorm(x, norm_g[l]) * (1 + scale_x[:, None, :]) + shift_x[:, None, :]
        hc = rmsnorm(xc, norm_g[l]) * (1 + scale_c) + shift_c
        update_ctx = l < last_ctx_reader
        i = l // 2
        if l % 2 == 0:
            yx, yc = attention_mixer(hx, hc, att_w_in[i], mla_q_norm[i], mla_w_uq[i], mla_kv_norm[i],
                                     mla_w_ukv[i], na_rpb[i], att_w_out[i], rope, update_ctx)
        else:
            yx = sgu_mixer(hx, sgu_w_in[i], sgu_ln_g[i], sgu_ln_b[i], sgu_w_s[i], sgu_b_s[i], sgu_w_out[i])
            yc = sgu_mixer(hc, sgu_w_in[i], sgu_ln_g[i], sgu_ln_b[i], sgu_w_s[i], sgu_b_s[i], sgu_w_out[i]) if update_ctx else None
        x = x + gate_x[:, None, :] * yx
        if update_ctx:
            xc = xc + gate_c * yc
    return rmsnorm(x, final_norm)
```

```python
import functools
import math

import numpy as np
import jax
import jax.numpy as jnp
from jax import lax
from jax.experimental import pallas as pl
from jax.experimental.pallas import tpu as pltpu

F32 = jnp.float32
BF16 = jnp.bfloat16

EPS = 1e-6
NEG = -1e30
LANES = 128
VMEM_LIMIT = 56 * 1024 * 1024

GRID_W = 64
MLA_HEADS = 8
MLA_NOPE = 64
MLA_ROPE = 32
MLA_Q_LORA = 768
MLA_KV_LORA = 256
MLA_WIDTH = 512
ROPE_BASE = 10000.0
NA_HEADS = 8
NA_HEAD_DIM = 64
NA_WIDTH = 512
NA_KH = 8
NA_KW = 16
SGU_CHUNK = 128
SGU_WIDTH = 2048
SGU_GROUPS = 8
SGU_GROUP_DIM = SGU_WIDTH // SGU_GROUPS

NA_QROWS = 8
NA_KROWS = 16


def _dot(a, b):
    return jnp.dot(a, b, preferred_element_type=F32)


def _dot_nt(a, b):
    return lax.dot_general(a, b, (((1,), (1,)), ((), ())), preferred_element_type=F32)


def _rms(x, g):
    return x * lax.rsqrt(jnp.mean(x * x, axis=-1, keepdims=True) + EPS) * g


def _silu(x):
    return x * jax.nn.sigmoid(x)


def _gelu(x):
    return 0.5 * x * (1.0 + lax.erf(x * np.float32(math.sqrt(0.5))))


def _params(n_axes, vmem=None):
    return pltpu.CompilerParams(dimension_semantics=("arbitrary",) * n_axes,
                                vmem_limit_bytes=vmem)


def _const_spec(shape):
    zeros = (0,) * len(shape)
    return pl.BlockSpec(shape, lambda *_: zeros, pipeline_mode=pl.Buffered(1))


def _mod_kernel(cond_ref, w_ref, b_ref, o_ref):
    s = _silu(cond_ref[...])
    w = w_ref[0]
    s_hi = s.astype(BF16)
    s_lo = (s - s_hi.astype(F32)).astype(BF16)
    w_hi = w.astype(BF16)
    w_lo = (w - w_hi.astype(F32)).astype(BF16)
    o_ref[0] = _dot(s_hi, w_hi) + _dot(s_hi, w_lo) + _dot(s_lo, w_hi) + b_ref[0]


def _mod_call(cond, w_mod, b_mod):
    depth, d, d3 = w_mod.shape
    tn = 1024
    return pl.pallas_call(
        _mod_kernel,
        grid=(depth, d3 // tn),
        in_specs=[pl.BlockSpec((8, d), lambda l, j: (0, 0)),
                  pl.BlockSpec((1, d, tn), lambda l, j: (l, 0, j)),
                  pl.BlockSpec((1, 1, tn), lambda l, j: (l, 0, j))],
        out_specs=pl.BlockSpec((1, 8, tn), lambda l, j: (l, 0, j)),
        out_shape=jax.ShapeDtypeStruct((depth, 8, d3), F32),
        compiler_params=_params(2),
        name="adaln_mod",
    )(cond, w_mod, b_mod.reshape(depth, 1, d3))


_C_CQ = 0
_C_CKV = _C_CQ + MLA_Q_LORA
_C_KR = _C_CKV + MLA_KV_LORA
_C_GM = _C_KR + LANES
_C_QN = _C_GM + MLA_WIDTH
_C_KN = _C_QN + NA_WIDTH
_C_VN = _C_KN + NA_WIDTH
_C_GN = _C_VN + NA_WIDTH
_C_END = _C_GN + NA_WIDTH


def _att_in_kernel(x_ref, mod_ref, ng_ref, win_ref, qnorm_ref, wuq_ref, kvnorm_ref, wk_ref, wv_ref,
                   cq_ref, sq_ref, ck_ref, sk_ref,
                   q_ref, k_ref, v_ref, qn_ref, kn_ref, vn_ref, sg_ref, *, row):
    d = x_ref.shape[1]
    shift = mod_ref[row:row + 1, 0:d]
    scale = mod_ref[row:row + 1, d:2 * d]
    hb = (_rms(x_ref[...], ng_ref[...]) * (1.0 + scale) + shift).astype(BF16)

    def proj(lo, hi):
        return _dot(hb, win_ref[:, lo:hi])

    cqn = _rms(proj(_C_CQ, _C_CKV), qnorm_ref[...]).astype(BF16)
    q = _dot(cqn, wuq_ref[...])
    q_sw = pltpu.roll(q, q.shape[1] - MLA_ROPE, 1)
    cq_t, sq_t = cq_ref[...], sq_ref[...]
    for h in range(MLA_HEADS):
        sl = slice(h * LANES, (h + 1) * LANES)
        q_ref[:, sl] = (q[:, sl] * cq_t + q_sw[:, sl] * sq_t).astype(BF16)

    ckvn = _rms(proj(_C_CKV, _C_KR), kvnorm_ref[...]).astype(BF16)
    k_nope = _dot(ckvn, wk_ref[...])
    kr = proj(_C_KR, _C_GM)
    kr_rot = kr * ck_ref[...] + pltpu.roll(kr, LANES - MLA_ROPE, 1) * sk_ref[...]
    for h in range(MLA_HEADS):
        sl = slice(h * LANES, (h + 1) * LANES)
        k_ref[:, sl] = (k_nope[:, sl] + kr_rot).astype(BF16)
    v_ref[...] = _dot(ckvn, wv_ref[...]).astype(BF16)

    qn_ref[...] = proj(_C_QN, _C_KN).astype(BF16)
    kn_ref[...] = proj(_C_KN, _C_VN).astype(BF16)
    vn_ref[...] = proj(_C_VN, _C_GN).astype(BF16)
    sg_ref[:, 0:MLA_WIDTH] = _silu(proj(_C_GM, _C_QN)).astype(BF16)
    sg_ref[:, MLA_WIDTH:] = _silu(proj(_C_GN, _C_END)).astype(BF16)


def _att_in_call(x, mod, ng, w, tabs, *, row, tm):
    n, d = x.shape
    hw = MLA_HEADS * LANES
    tok = lambda width: pl.BlockSpec((tm, width), lambda i: (i, 0))
    out_widths = (hw, hw, MLA_WIDTH, NA_WIDTH, NA_WIDTH, NA_WIDTH, MLA_WIDTH + NA_WIDTH)
    return pl.pallas_call(
        functools.partial(_att_in_kernel, row=row),
        grid=(n // tm,),
        in_specs=[tok(d), _const_spec(mod.shape), _const_spec(ng.shape),
                  _const_spec(w["w_in"].shape), _const_spec(w["q_norm"].shape),
                  _const_spec(w["w_uq"].shape), _const_spec(w["kv_norm"].shape),
                  _const_spec(w["w_k"].shape), _const_spec(w["w_v"].shape),
                  tok(LANES), tok(LANES), tok(LANES), tok(LANES)],
        out_specs=[tok(wd) for wd in out_widths],
        out_shape=[jax.ShapeDtypeStruct((n, wd), BF16) for wd in out_widths],
        compiler_params=_params(1, VMEM_LIMIT),
        name="att_in_ctx" if row else "att_in",
    )(x, mod, ng, w["w_in"], w["q_norm"], w["w_uq"], w["kv_norm"], w["w_k"], w["w_v"], *tabs)


def _softmax_step(s_list, v_list, m_ref, l_ref, acc_ref):
    m_old = m_ref[...]
    m_new = m_old
    for s in s_list:
        m_new = jnp.maximum(m_new, jnp.max(s, axis=1, keepdims=True))
    alpha = jnp.exp(m_old - m_new)
    l_new = alpha * l_ref[...]
    acc = alpha * acc_ref[...]
    for s, v in zip(s_list, v_list):
        p = jnp.exp(s - m_new)
        l_new = l_new + jnp.sum(p, axis=1, keepdims=True)
        acc = acc + _dot(p.astype(BF16), v)
    m_ref[...] = m_new
    l_ref[...] = l_new
    acc_ref[...] = acc


def _merge_pair(o_even, o_odd):
    lane = lax.broadcasted_iota(jnp.int32, o_even.shape, 1)
    return jnp.where(lane < NA_HEAD_DIM, o_even, o_odd)


def _mla_kernel(*refs, tk, n_chunks):
    if n_chunks:
        q_ref, kx_ref, vx_ref, kc_ref, vc_ref, o_ref, m_ref, l_ref, acc_ref = refs
    else:
        q_ref, kc_ref, vc_ref, o_ref, m_ref, l_ref, acc_ref = refs
    outs = []
    for e in range(2):
        hs = slice(e * LANES, (e + 1) * LANES)
        qe = q_ref[:, hs]
        m_ref[...] = jnp.full(m_ref.shape, NEG, F32)
        l_ref[...] = jnp.zeros(l_ref.shape, F32)
        acc_ref[...] = jnp.zeros(acc_ref.shape, F32)

        if n_chunks:
            def body(j, carry):
                st = pl.multiple_of(j * tk, tk)
                s = _dot_nt(qe, kx_ref[pl.ds(st, tk), hs])
                _softmax_step([s], [vx_ref[pl.ds(st, tk), :]], m_ref, l_ref, acc_ref)
                return carry
            lax.fori_loop(0, n_chunks, body, 0)
        _softmax_step([_dot_nt(qe, kc_ref[:, hs])], [vc_ref[...]], m_ref, l_ref, acc_ref)
        outs.append(acc_ref[...] / l_ref[...])
    o_ref[...] = _merge_pair(*outs).astype(BF16)


def _mla_call(q, kx, vx, kc, vc, *, tq, tk):
    n = q.shape[0]
    lc = kc.shape[0]
    pairs = MLA_HEADS // 2
    in_specs = [pl.BlockSpec((tq, 2 * LANES), lambda hp, i: (i, hp))]
    args = [q]
    n_chunks = 0
    if kx is not None:
        s = kx.shape[0]
        n_chunks = s // tk
        in_specs += [pl.BlockSpec((s, 2 * LANES), lambda hp, i: (0, hp)),
                     pl.BlockSpec((s, LANES), lambda hp, i: (0, hp))]
        args += [kx, vx]
    in_specs += [pl.BlockSpec((lc, 2 * LANES), lambda hp, i: (0, hp)),
                 pl.BlockSpec((lc, LANES), lambda hp, i: (0, hp))]
    args += [kc, vc]
    return pl.pallas_call(
        functools.partial(_mla_kernel, tk=tk, n_chunks=n_chunks),
        grid=(pairs, n // tq),
        in_specs=in_specs,
        out_specs=pl.BlockSpec((tq, LANES), lambda hp, i: (i, hp)),
        out_shape=jax.ShapeDtypeStruct((n, MLA_WIDTH), BF16),
        scratch_shapes=[pltpu.VMEM((tq, 1), F32), pltpu.VMEM((tq, 1), F32),
                        pltpu.VMEM((tq, LANES), F32)],
        compiler_params=_params(2, VMEM_LIMIT),
        name="mla_flash" if kx is not None else "mla_ctx",
    )(*args)


def _pair_masks(shape):
    lane = lax.broadcasted_iota(jnp.int32, shape, 1)
    return lane < NA_HEAD_DIM, lane >= NA_HEAD_DIM


def _na_kernel(q_ref, kn_ref, vn_ref, kc_ref, vc_ref, bias_ref, o_ref, m_ref, l_ref, acc_ref, *, n_tok):
    tq = q_ref.shape[0]
    tkw = NA_KROWS * GRID_W
    b = pl.program_id(1)
    start = jnp.clip(b * tq - (NA_KH // 2) * GRID_W, 0, n_tok - tkw)
    start = pl.multiple_of(start, (NA_KH // 2) * GRID_W)
    k_win = kn_ref[pl.ds(start, tkw), :]
    v_win = vn_ref[pl.ds(start, tkw), :]
    q = q_ref[...]
    outs = []
    for e, mask in enumerate(_pair_masks(q.shape)):
        qe = jnp.where(mask, q, jnp.zeros_like(q))
        m_ref[...] = jnp.full(m_ref.shape, NEG, F32)
        l_ref[...] = jnp.zeros(l_ref.shape, F32)
        acc_ref[...] = jnp.zeros(acc_ref.shape, F32)
        s_win = _dot_nt(qe, k_win) + bias_ref[0, e]
        s_ctx = _dot_nt(qe, kc_ref[...])
        _softmax_step([s_win, s_ctx], [v_win, vc_ref[...]], m_ref, l_ref, acc_ref)
        outs.append(acc_ref[...] / l_ref[...])
    o_ref[...] = _merge_pair(*outs).astype(BF16)


def _na_call(qn, kn, vn, knc, vnc, bias):
    n = qn.shape[0]
    lc = knc.shape[0]
    tq = NA_QROWS * GRID_W
    tkw = NA_KROWS * GRID_W
    nb = n // tq
    pairs = NA_HEADS // 2

    def bias_map(hp, b):
        return (jnp.where(b == 0, 0, jnp.where(b == nb - 1, 2, 1)), hp, 0, 0)

    return pl.pallas_call(
        functools.partial(_na_kernel, n_tok=n),
        grid=(pairs, nb),
        in_specs=[pl.BlockSpec((tq, LANES), lambda hp, b: (b, hp)),
                  pl.BlockSpec((n, LANES), lambda hp, b: (0, hp)),
                  pl.BlockSpec((n, LANES), lambda hp, b: (0, hp)),
                  pl.BlockSpec((lc, LANES), lambda hp, b: (0, hp)),
                  pl.BlockSpec((lc, LANES), lambda hp, b: (0, hp)),
                  pl.BlockSpec((1, 2, tq, tkw), bias_map)],
        out_specs=pl.BlockSpec((tq, LANES), lambda hp, b: (b, hp)),
        out_shape=jax.ShapeDtypeStruct((n, NA_WIDTH), BF16),
        scratch_shapes=[pltpu.VMEM((tq, 1), F32), pltpu.VMEM((tq, 1), F32),
                        pltpu.VMEM((tq, LANES), F32)],
        compiler_params=_params(2, VMEM_LIMIT),
        name="na_window",
    )(qn, kn, vn, knc, vnc, bias)


def _pair_attn_kernel(q_ref, k_ref, v_ref, o_ref, m_ref, l_ref, acc_ref):
    q = q_ref[...]
    outs = []
    for mask in _pair_masks(q.shape):
        qe = jnp.where(mask, q, jnp.zeros_like(q))
        m_ref[...] = jnp.full(m_ref.shape, NEG, F32)
        l_ref[...] = jnp.zeros(l_ref.shape, F32)
        acc_ref[...] = jnp.zeros(acc_ref.shape, F32)
        _softmax_step([_dot_nt(qe, k_ref[...])], [v_ref[...]], m_ref, l_ref, acc_ref)
        outs.append(acc_ref[...] / l_ref[...])
    o_ref[...] = _merge_pair(*outs).astype(BF16)


def _pair_attn_call(q, k, v):
    n = q.shape[0]
    spec = lambda rows: pl.BlockSpec((rows, LANES), lambda hp: (0, hp))
    return pl.pallas_call(
        _pair_attn_kernel,
        grid=(NA_HEADS // 2,),
        in_specs=[spec(n), spec(k.shape[0]), spec(v.shape[0])],
        out_specs=spec(n),
        out_shape=jax.ShapeDtypeStruct((n, NA_WIDTH), BF16),
        scratch_shapes=[pltpu.VMEM((n, 1), F32), pltpu.VMEM((n, 1), F32),
                        pltpu.VMEM((n, LANES), F32)],
        compiler_params=_params(1),
        name="na_ctx",
    )(q, k, v)


def _na_bias_tables(rpb, rows):
    w, kw, kh = GRID_W, NA_KW, NA_KH
    c = np.arange(w)
    c0 = np.clip(c - kw // 2, 0, w - kw)
    kc = np.arange(w)
    col_valid = (kc[None, :] >= c0[:, None]) & (kc[None, :] < c0[:, None] + kw)
    col_idx = np.clip(kc[None, :] - c[:, None] + (kw - 1), 0, 2 * kw - 2)
    by_col = jnp.where(col_valid[None, None], rpb[:, :, col_idx], NEG)

    q_row0 = np.array([0, NA_QROWS, rows - NA_QROWS])
    k_row0 = np.clip(q_row0 - kh // 2, 0, rows - NA_KROWS)
    r = q_row0[:, None, None] + np.arange(NA_QROWS)[None, :, None]
    kr = k_row0[:, None, None] + np.arange(NA_KROWS)[None, None, :]
    r0 = np.clip(r - kh // 2, 0, rows - kh)
    row_valid = (kr >= r0) & (kr < r0 + kh)
    row_idx = np.clip(kr - r + (kh - 1), 0, 2 * kh - 2)
    t = by_col[:, row_idx]
    t = jnp.where(row_valid[None, :, :, :, None, None], t, NEG)
    h = rpb.shape[0]
    return t.transpose(1, 0, 2, 4, 3, 5).reshape(3, h, NA_QROWS * w, NA_KROWS * w)


def _att_out_kernel(mla_ref, na_ref, sg_ref, x_ref, mod_ref, w_ref, o_ref, *, row):
    d = x_ref.shape[1]
    a = (mla_ref[...].astype(F32) * sg_ref[:, 0:MLA_WIDTH].astype(F32)).astype(BF16)
    b = (na_ref[...].astype(F32) * sg_ref[:, MLA_WIDTH:].astype(F32)).astype(BF16)
    y = _dot(a, w_ref[0:MLA_WIDTH, :]) + _dot(b, w_ref[MLA_WIDTH:, :])
    o_ref[...] = x_ref[...] + mod_ref[row:row + 1, 2 * d:3 * d] * y


def _att_out_call(mla, na, sg, x, mod, w_out, *, row, tm):
    n, d = x.shape
    tok = lambda width: pl.BlockSpec((tm, width), lambda i: (i, 0))
    return pl.pallas_call(
        functools.partial(_att_out_kernel, row=row),
        grid=(n // tm,),
        in_specs=[tok(MLA_WIDTH), tok(NA_WIDTH), tok(MLA_WIDTH + NA_WIDTH), tok(d),
                  _const_spec(mod.shape), _const_spec(w_out.shape)],
        out_specs=tok(d),
        out_shape=jax.ShapeDtypeStruct((n, d), F32),
        compiler_params=_params(1, VMEM_LIMIT),
        name="att_out_ctx" if row else "att_out",
    )(mla, na, sg, x, mod, w_out)


def _sgu_kernel(x_ref, mod_ref, ng_ref, win_ref, lng_ref, lnb_ref, ws_ref, bs_ref, wout_ref, fn_ref,
                o_ref, *, row, final):
    tm, d = x_ref.shape
    x = x_ref[...]
    shift = mod_ref[row:row + 1, 0:d]
    scale = mod_ref[row:row + 1, d:2 * d]
    gate = mod_ref[row:row + 1, 2 * d:3 * d]
    hb = (_rms(x, ng_ref[...]) * (1.0 + scale) + shift).astype(BF16)

    v = _gelu(_dot(hb, win_ref[:, SGU_WIDTH:2 * SGU_WIDTH]))
    mu = jnp.mean(v, axis=-1, keepdims=True)
    vc = v - mu
    var = jnp.mean(vc * vc, axis=-1, keepdims=True)
    vn = (vc * lax.rsqrt(var + EPS) * lng_ref[...] + lnb_ref[...]).astype(BF16)

    y = jnp.zeros((tm, d), F32)
    for g in range(SGU_GROUPS):
        cols = slice(g * SGU_GROUP_DIM, (g + 1) * SGU_GROUP_DIM)
        u = _gelu(_dot(hb, win_ref[:, g * SGU_GROUP_DIM:(g + 1) * SGU_GROUP_DIM]))
        gt = _silu(_dot(hb, win_ref[:, 2 * SGU_WIDTH + g * SGU_GROUP_DIM:
                                    2 * SGU_WIDTH + (g + 1) * SGU_GROUP_DIM]))
        sv = jnp.concatenate(
            [_dot(ws_ref[g], vn[c * SGU_CHUNK:(c + 1) * SGU_CHUNK, cols]) + bs_ref[g]
             for c in range(tm // SGU_CHUNK)], axis=0)
        y = y + _dot((u * sv * gt).astype(BF16), wout_ref[cols, :])
    out = x + gate * y
    if final:
        out = _rms(out, fn_ref[...])
    o_ref[...] = out


def _sgu_call(x, mod, ng, w, final_norm, *, row, tm, final):
    n, d = x.shape
    tok = pl.BlockSpec((tm, d), lambda i: (i, 0))
    consts = [mod, ng, w["w_in"], w["ln_g"], w["ln_b"], w["w_s"], w["b_s"], w["w_out"], final_norm]
    return pl.pallas_call(
        functools.partial(_sgu_kernel, row=row, final=final),
        grid=(n // tm,),
        in_specs=[tok] + [_const_spec(a.shape) for a in consts],
        out_specs=tok,
        out_shape=jax.ShapeDtypeStruct((n, d), F32),
        compiler_params=_params(1, VMEM_LIMIT),
        name="sgu_ctx" if row else "sgu",
    )(x, *consts)


def _pack_att_weights(w_in, q_norm, w_uq, kv_norm, w_ukv, w_out):
    d = w_in.shape[0]
    bounds = np.cumsum([0, MLA_Q_LORA, MLA_KV_LORA, MLA_ROPE, MLA_WIDTH, NA_WIDTH, NA_WIDTH, NA_WIDTH, NA_WIDTH])
    cq, ckv, kr, gm, qn, kn, vn, gn = [w_in[:, a:b] for a, b in zip(bounds[:-1], bounds[1:])]
    half = MLA_ROPE // 2
    x1, x2 = kr[:, :half], kr[:, half:]
    kr_group = jnp.concatenate([jnp.zeros((d, MLA_NOPE), F32), x1, x2, x2, x1], axis=1)
    na_scale = NA_HEAD_DIM ** -0.5
    packed = jnp.concatenate([cq, ckv, kr_group, gm, qn * na_scale, kn, vn, gn], axis=1)
    assert packed.shape[1] == _C_END

    uq = w_uq.reshape(MLA_Q_LORA, MLA_HEADS, MLA_NOPE + MLA_ROPE)
    q1, q2 = uq[..., MLA_NOPE:MLA_NOPE + half], uq[..., MLA_NOPE + half:]
    uq = jnp.concatenate([uq[..., :MLA_NOPE], q1, q2, q2, q1], axis=-1).reshape(MLA_Q_LORA, MLA_HEADS * LANES)

    ukv = w_ukv.reshape(MLA_KV_LORA, MLA_HEADS, MLA_NOPE + 64)
    wk = jnp.concatenate([ukv[..., :MLA_NOPE], jnp.zeros_like(ukv[..., :LANES - MLA_NOPE])], axis=-1)
    wk = wk.reshape(MLA_KV_LORA, MLA_HEADS * LANES)
    wv = ukv[..., MLA_NOPE:].reshape(MLA_KV_LORA, MLA_WIDTH)
    return {"w_in": packed.astype(BF16), "q_norm": q_norm.reshape(1, -1), "w_uq": uq.astype(BF16),
            "kv_norm": kv_norm.reshape(1, -1), "w_k": wk.astype(BF16), "w_v": wv.astype(BF16),
            "w_out": w_out.astype(BF16)}


def _rope_tables(s, lc):
    t = jnp.arange(s)
    row = (t // GRID_W).astype(F32)
    col = (t % GRID_W).astype(F32)
    axis_dims = MLA_ROPE // 2
    inv = jnp.power(ROPE_BASE, -jnp.arange(0, axis_dims, 2, dtype=F32) / axis_dims)
    ang = jnp.concatenate([row[:, None] * inv, col[:, None] * inv], axis=-1)
    cos, sin = jnp.cos(ang), jnp.sin(ang)
    scale = (MLA_NOPE + MLA_ROPE) ** -0.5
    z = lambda n, w: jnp.zeros((n, w), F32)
    o = lambda n, w: jnp.ones((n, w), F32)
    pad = LANES - MLA_NOPE - MLA_ROPE
    cos_q = jnp.concatenate([o(s, MLA_NOPE), cos, cos, z(s, pad)], axis=1) * scale
    sin_q = jnp.concatenate([z(s, MLA_NOPE), -sin, sin, z(s, pad)], axis=1) * scale
    cos_k = jnp.concatenate([z(s, MLA_NOPE), cos, cos, z(s, pad)], axis=1)
    sin_k = jnp.concatenate([z(s, MLA_NOPE), -sin, sin, z(s, pad)], axis=1)
    ctx_q = jnp.concatenate([o(lc, MLA_NOPE + MLA_ROPE), z(lc, pad)], axis=1) * scale
    ctx_k = jnp.concatenate([z(lc, MLA_NOPE), o(lc, MLA_ROPE), z(lc, pad)], axis=1)
    return (cos_q, sin_q, cos_k, sin_k), (ctx_q, z(lc, LANES), ctx_k, z(lc, LANES))


def kernel(x, c, ctx, c_ctx, norm_g, w_mod, b_mod, att_w_in, mla_q_norm, mla_w_uq, mla_kv_norm, mla_w_ukv,
           na_rpb, att_w_out, sgu_w_in, sgu_ln_g, sgu_ln_b, sgu_w_s, sgu_b_s, sgu_w_out, final_norm):
    batch, s, d = x.shape
    lc = ctx.shape[1]
    depth = norm_g.shape[0]
    assert batch == 1 and c.shape[0] == 1
    rows = s // GRID_W
    assert s % (NA_QROWS * GRID_W) == 0 and rows >= NA_KROWS

    tm = 512 if s % 512 == 0 else 256
    tq = 512 if s % 512 == 0 else 256
    tk = 1024 if s % 1024 == 0 else 512

    cond = jnp.concatenate([c, c_ctx[None, :], jnp.zeros((6, d), F32)], axis=0)
    mods = _mod_call(cond, w_mod, b_mod)
    tabs_x, tabs_c = _rope_tables(s, lc)
    fnorm = final_norm.reshape(1, d)

    xs, xc = x[0], ctx[0]
    last_ctx_reader = max(l for l in range(depth) if l % 2 == 0)
    for l in range(depth):
        i = l // 2
        update_ctx = l < last_ctx_reader
        ng = norm_g[l].reshape(1, d)
        if l % 2 == 0:
            w = _pack_att_weights(att_w_in[i], mla_q_norm[i], mla_w_uq[i], mla_kv_norm[i], mla_w_ukv[i],
                                  att_w_out[i])
            bias = _na_bias_tables(na_rpb[i], rows)
            qx, kx, vx, qnx, knx, vnx, sgx = _att_in_call(xs, mods[l], ng, w, tabs_x, row=0, tm=tm)
            qc, kc, vc, qnc, knc, vnc, sgc = _att_in_call(xc, mods[l], ng, w, tabs_c, row=1, tm=lc)
            mla_x = _mla_call(qx, kx, vx, kc, vc, tq=tq, tk=tk)
            na_x = _na_call(qnx, knx, vnx, knc, vnc, bias)
            xs = _att_out_call(mla_x, na_x, sgx, xs, mods[l], w["w_out"], row=0, tm=tm)
            if update_ctx:
                mla_c = _mla_call(qc, None, None, kc, vc, tq=lc, tk=tk)
                na_c = _pair_attn_call(qnc, knc, vnc)
                xc = _att_out_call(mla_c, na_c, sgc, xc, mods[l], w["w_out"], row=1, tm=lc)
        else:
            w = {"w_in": sgu_w_in[i].astype(BF16), "ln_g": sgu_ln_g[i].reshape(1, -1),
                 "ln_b": sgu_ln_b[i].reshape(1, -1), "w_s": sgu_w_s[i].astype(BF16),
                 "b_s": sgu_b_s[i][:, :, None], "w_out": sgu_w_out[i].astype(BF16)}
            final = l == depth - 1
            xs = _sgu_call(xs, mods[l], ng, w, fnorm, row=0, tm=tm, final=final)
            if update_ctx:
                xc = _sgu_call(xc, mods[l], ng, w, fnorm, row=1, tm=lc, final=False)
    if depth % 2 == 1:
        raise NotImplementedError("final RMSNorm is fused into a trailing spatial-gating layer")
    return xs[None]
```

```python
import functools
import math

import numpy as np
import jax
import jax.numpy as jnp
from jax import lax
from jax.experimental import pallas as pl
from jax.experimental.pallas import tpu as pltpu

F32 = jnp.float32
BF16 = jnp.bfloat16

EPS = 1e-6
NEG = -1e30
LANES = 128
VMEM_LIMIT = 56 * 1024 * 1024

GRID_W = 64
MLA_HEADS = 8
MLA_NOPE = 64
MLA_ROPE = 32
MLA_V = 64
MLA_Q_LORA = 768
MLA_KV_LORA = 256
MLA_WIDTH = 512
ROPE_BASE = 10000.0
NA_HEADS = 8
NA_HEAD_DIM = 64
NA_WIDTH = 512
NA_KH = 8
NA_KW = 16
SGU_CHUNK = 128
SGU_WIDTH = 2048
SGU_GROUPS = 8
SGU_GROUP_DIM = SGU_WIDTH // SGU_GROUPS

NA_QROWS = 8
NA_KROWS = 16


def _dot(a, b):
    return jnp.dot(a, b, preferred_element_type=F32)


def _dot_nt(a, b):
    return lax.dot_general(a, b, (((1,), (1,)), ((), ())), preferred_element_type=F32)


def _rms(x, g):
    return x * lax.rsqrt(jnp.mean(x * x, axis=-1, keepdims=True) + EPS) * g


def _silu(x):
    return x * jax.nn.sigmoid(x)


def _gelu(x):
    return 0.5 * x * (1.0 + lax.erf(x * np.float32(math.sqrt(0.5))))


def _params(n_axes, vmem=None):
    return pltpu.CompilerParams(dimension_semantics=("arbitrary",) * n_axes,
                                vmem_limit_bytes=vmem)


def _const_spec(shape):
    zeros = (0,) * len(shape)
    return pl.BlockSpec(shape, lambda *_: zeros, pipeline_mode=pl.Buffered(1))


def _mod_kernel(cond_ref, w_ref, b_ref, o_ref):
    s = _silu(cond_ref[...])
    w = w_ref[0]
    s_hi = s.astype(BF16)
    s_lo = (s - s_hi.astype(F32)).astype(BF16)
    w_hi = w.astype(BF16)
    w_lo = (w - w_hi.astype(F32)).astype(BF16)
    o_ref[0] = _dot(s_hi, w_hi) + _dot(s_hi, w_lo) + _dot(s_lo, w_hi) + b_ref[0]


def _mod_call(cond, w_mod, b_mod):
    depth, d, d3 = w_mod.shape
    tn = 1024
    return pl.pallas_call(
        _mod_kernel,
        grid=(depth, d3 // tn),
        in_specs=[pl.BlockSpec((8, d), lambda l, j: (0, 0)),
                  pl.BlockSpec((1, d, tn), lambda l, j: (l, 0, j)),
                  pl.BlockSpec((1, 1, tn), lambda l, j: (l, 0, j))],
        out_specs=pl.BlockSpec((1, 8, tn), lambda l, j: (l, 0, j)),
        out_shape=jax.ShapeDtypeStruct((depth, 8, d3), F32),
        compiler_params=_params(2),
        name="adaln_mod",
    )(cond, w_mod, b_mod.reshape(depth, 1, d3))


_C_CQ = 0
_C_CKV = _C_CQ + MLA_Q_LORA
_C_KR = _C_CKV + MLA_KV_LORA
_C_GM = _C_KR + LANES
_C_QN = _C_GM + MLA_WIDTH
_C_KN = _C_QN + NA_WIDTH
_C_VN = _C_KN + NA_WIDTH
_C_GN = _C_VN + NA_WIDTH
_C_END = _C_GN + NA_WIDTH


def _att_in_kernel(x_ref, mod_ref, ng_ref, win_ref, qnorm_ref, wuq_ref, kvnorm_ref, wk_ref, wv_ref,
                   cq_ref, sq_ref, ck_ref, sk_ref,
                   qt_ref, k_ref, vt_ref, qn_ref, kn_ref, vn_ref, sg_ref, *, row):
    d = x_ref.shape[1]
    shift = mod_ref[row:row + 1, 0:d]
    scale = mod_ref[row:row + 1, d:2 * d]
    hb = (_rms(x_ref[...], ng_ref[...]) * (1.0 + scale) + shift).astype(BF16)

    def proj(lo, hi):
        return _dot(hb, win_ref[:, lo:hi])

    cqn = _rms(proj(_C_CQ, _C_CKV), qnorm_ref[...]).astype(BF16)
    q = _dot(cqn, wuq_ref[...])
    q_sw = pltpu.roll(q, q.shape[1] - MLA_ROPE, 1)
    cq_t, sq_t = cq_ref[...], sq_ref[...]
    for h in range(MLA_HEADS):
        sl = slice(h * LANES, (h + 1) * LANES)
        qt_ref[sl, :] = (q[:, sl] * cq_t + q_sw[:, sl] * sq_t).T.astype(BF16)

    ckvn = _rms(proj(_C_CKV, _C_KR), kvnorm_ref[...]).astype(BF16)
    k_nope = _dot(ckvn, wk_ref[...])
    kr = proj(_C_KR, _C_GM)
    kr_rot = kr * ck_ref[...] + pltpu.roll(kr, LANES - MLA_ROPE, 1) * sk_ref[...]
    for h in range(MLA_HEADS):
        sl = slice(h * LANES, (h + 1) * LANES)
        k_ref[:, sl] = (k_nope[:, sl] + kr_rot).astype(BF16)
    vt_ref[...] = _dot(ckvn, wv_ref[...]).T.astype(BF16)

    qn_ref[...] = proj(_C_QN, _C_KN).astype(BF16)
    kn_ref[...] = proj(_C_KN, _C_VN).astype(BF16)
    vn_ref[...] = proj(_C_VN, _C_GN).astype(BF16)
    sg_ref[:, 0:MLA_WIDTH] = _silu(proj(_C_GM, _C_QN)).astype(BF16)
    sg_ref[:, MLA_WIDTH:] = _silu(proj(_C_GN, _C_END)).astype(BF16)


def _att_in_call(x, mod, ng, w, tabs, *, row, tm):
    n, d = x.shape
    hw = MLA_HEADS * LANES
    tok = lambda width: pl.BlockSpec((tm, width), lambda i: (i, 0))
    tok_t = lambda width: pl.BlockSpec((width, tm), lambda i: (0, i))
    out_widths = (hw, hw, MLA_WIDTH, NA_WIDTH, NA_WIDTH, NA_WIDTH, MLA_WIDTH + NA_WIDTH)
    transposed = (True, False, True, False, False, False, False)
    return pl.pallas_call(
        functools.partial(_att_in_kernel, row=row),
        grid=(n // tm,),
        in_specs=[tok(d), _const_spec(mod.shape), _const_spec(ng.shape),
                  _const_spec(w["w_in"].shape), _const_spec(w["q_norm"].shape),
                  _const_spec(w["w_uq"].shape), _const_spec(w["kv_norm"].shape),
                  _const_spec(w["w_k"].shape), _const_spec(w["w_v"].shape),
                  tok(LANES), tok(LANES), tok(LANES), tok(LANES)],
        out_specs=[tok_t(wd) if t else tok(wd) for wd, t in zip(out_widths, transposed)],
        out_shape=[jax.ShapeDtypeStruct((wd, n) if t else (n, wd), BF16)
                   for wd, t in zip(out_widths, transposed)],
        compiler_params=_params(1, VMEM_LIMIT),
        name="att_in_ctx" if row else "att_in",
    )(x, mod, ng, w["w_in"], w["q_norm"], w["w_uq"], w["kv_norm"], w["w_k"], w["w_v"], *tabs)


def _softmax_step(s_list, v_list, m_ref, l_ref, acc_ref):
    m_old = m_ref[...]
    m_new = m_old
    for s in s_list:
        m_new = jnp.maximum(m_new, jnp.max(s, axis=1, keepdims=True))
    alpha = jnp.exp(m_old - m_new)
    l_new = alpha * l_ref[...]
    acc = alpha * acc_ref[...]
    for s, v in zip(s_list, v_list):
        p = jnp.exp(s - m_new)
        l_new = l_new + jnp.sum(p, axis=1, keepdims=True)
        acc = acc + _dot(p.astype(BF16), v)
    m_ref[...] = m_new
    l_ref[...] = l_new
    acc_ref[...] = acc


def _merge_pair(o_even, o_odd):
    lane = lax.broadcasted_iota(jnp.int32, o_even.shape, 1)
    return jnp.where(lane < NA_HEAD_DIM, o_even, o_odd)


def _mla_kernel(*refs, tk, n_chunks):
    if n_chunks:
        qt_ref, kx_ref, vxt_ref, kc_ref, vct_ref, o_ref, acc_ref, sa_ref, sb_ref = refs
    else:
        qt_ref, kc_ref, vct_ref, o_ref, acc_ref = refs
    tq = qt_ref.shape[1]
    outs = []
    for e in range(2):
        hs = slice(e * LANES, (e + 1) * LANES)
        vs = slice(e * MLA_V, (e + 1) * MLA_V)
        qt = qt_ref[hs, :]

        def scores(j):
            return _dot(kx_ref[pl.ds(pl.multiple_of(j * tk, tk), tk), hs], qt)

        def values(j):
            return vxt_ref[vs, pl.ds(pl.multiple_of(j * tk, tk), tk)]

        def update(s, vt, m, l):
            m_new = jnp.maximum(m, jnp.max(s, axis=0, keepdims=True))
            alpha = jnp.exp2(m - m_new)
            p = jnp.exp2(s - m_new)
            l_new = alpha * l + jnp.sum(p, axis=0, keepdims=True)
            acc_ref[...] = alpha * acc_ref[...] + _dot(vt, p.astype(BF16))
            return m_new, l_new

        acc_ref[...] = jnp.zeros(acc_ref.shape, F32)
        m = jnp.full((1, tq), NEG, F32)
        l = jnp.zeros((1, tq), F32)
        if n_chunks:
            sa_ref[...] = scores(0)

            def body(t, carry):
                sb_ref[...] = scores(2 * t + 1)
                carry = update(sa_ref[...], values(2 * t), *carry)
                sa_ref[...] = scores(2 * t + 2)
                return update(sb_ref[...], values(2 * t + 1), *carry)
            m, l = lax.fori_loop(0, n_chunks // 2 - 1, body, (m, l))
            sb_ref[...] = scores(n_chunks - 1)
            m, l = update(sa_ref[...], values(n_chunks - 2), m, l)
            s_ctx = _dot(kc_ref[:, hs], qt)
            m, l = update(sb_ref[...], values(n_chunks - 1), m, l)
        else:
            s_ctx = _dot(kc_ref[:, hs], qt)
        m, l = update(s_ctx, vct_ref[vs, :], m, l)
        outs.append(acc_ref[...] / l)
    o_ref[...] = jnp.concatenate(outs, axis=0).T.astype(BF16)


def _mla_call(qt, kx, vxt, kc, vct, *, tq, tk):
    n = qt.shape[1]
    lc = kc.shape[0]
    pairs = MLA_HEADS // 2
    in_specs = [pl.BlockSpec((2 * LANES, tq), lambda hp, i: (hp, i))]
    args = [qt]
    n_chunks = 0
    if kx is not None:
        s = kx.shape[0]
        n_chunks = s // tk
        assert s % tk == 0 and n_chunks % 2 == 0
        in_specs += [pl.BlockSpec((s, 2 * LANES), lambda hp, i: (0, hp)),
                     pl.BlockSpec((LANES, s), lambda hp, i: (hp, 0))]
        args += [kx, vxt]
    in_specs += [pl.BlockSpec((lc, 2 * LANES), lambda hp, i: (0, hp)),
                 pl.BlockSpec((LANES, lc), lambda hp, i: (hp, 0))]
    args += [kc, vct]
    return pl.pallas_call(
        functools.partial(_mla_kernel, tk=tk, n_chunks=n_chunks),
        grid=(pairs, n // tq),
        in_specs=in_specs,
        out_specs=pl.BlockSpec((tq, LANES), lambda hp, i: (i, hp)),
        out_shape=jax.ShapeDtypeStruct((n, MLA_WIDTH), BF16),
        scratch_shapes=[pltpu.VMEM((MLA_V, tq), F32)] + [pltpu.VMEM((tk, tq), F32)] * (2 if n_chunks else 0),
        compiler_params=_params(2, VMEM_LIMIT),
        name="mla_flash" if kx is not None else "mla_ctx",
    )(*args)


def _pair_masks(shape):
    lane = lax.broadcasted_iota(jnp.int32, shape, 1)
    return lane < NA_HEAD_DIM, lane >= NA_HEAD_DIM


def _na_kernel(q_ref, kn_ref, vn_ref, kc_ref, vc_ref, bias_ref, o_ref, m_ref, l_ref, acc_ref, *, n_tok):
    tq = q_ref.shape[0]
    tkw = NA_KROWS * GRID_W
    b = pl.program_id(1)
    start = jnp.clip(b * tq - (NA_KH // 2) * GRID_W, 0, n_tok - tkw)
    start = pl.multiple_of(start, (NA_KH // 2) * GRID_W)
    k_win = kn_ref[pl.ds(start, tkw), :]
    v_win = vn_ref[pl.ds(start, tkw), :]
    q = q_ref[...]
    outs = []
    for e, mask in enumerate(_pair_masks(q.shape)):
        qe = jnp.where(mask, q, jnp.zeros_like(q))
        m_ref[...] = jnp.full(m_ref.shape, NEG, F32)
        l_ref[...] = jnp.zeros(l_ref.shape, F32)
        acc_ref[...] = jnp.zeros(acc_ref.shape, F32)
        s_win = _dot_nt(qe, k_win) + bias_ref[0, e]
        s_ctx = _dot_nt(qe, kc_ref[...])
        _softmax_step([s_win, s_ctx], [v_win, vc_ref[...]], m_ref, l_ref, acc_ref)
        outs.append(acc_ref[...] / l_ref[...])
    o_ref[...] = _merge_pair(*outs).astype(BF16)


def _na_call(qn, kn, vn, knc, vnc, bias):
    n = qn.shape[0]
    lc = knc.shape[0]
    tq = NA_QROWS * GRID_W
    tkw = NA_KROWS * GRID_W
    nb = n // tq
    pairs = NA_HEADS // 2

    def bias_map(hp, b):
        return (jnp.where(b == 0, 0, jnp.where(b == nb - 1, 2, 1)), hp, 0, 0)

    return pl.pallas_call(
        functools.partial(_na_kernel, n_tok=n),
        grid=(pairs, nb),
        in_specs=[pl.BlockSpec((tq, LANES), lambda hp, b: (b, hp)),
                  pl.BlockSpec((n, LANES), lambda hp, b: (0, hp)),
                  pl.BlockSpec((n, LANES), lambda hp, b: (0, hp)),
                  pl.BlockSpec((lc, LANES), lambda hp, b: (0, hp)),
                  pl.BlockSpec((lc, LANES), lambda hp, b: (0, hp)),
                  pl.BlockSpec((1, 2, tq, tkw), bias_map)],
        out_specs=pl.BlockSpec((tq, LANES), lambda hp, b: (b, hp)),
        out_shape=jax.ShapeDtypeStruct((n, NA_WIDTH), BF16),
        scratch_shapes=[pltpu.VMEM((tq, 1), F32), pltpu.VMEM((tq, 1), F32),
                        pltpu.VMEM((tq, LANES), F32)],
        compiler_params=_params(2, VMEM_LIMIT),
        name="na_window",
    )(qn, kn, vn, knc, vnc, bias)


def _pair_attn_kernel(q_ref, k_ref, v_ref, o_ref, m_ref, l_ref, acc_ref):
    q = q_ref[...]
    outs = []
    for mask in _pair_masks(q.shape):
        qe = jnp.where(mask, q, jnp.zeros_like(q))
        m_ref[...] = jnp.full(m_ref.shape, NEG, F32)
        l_ref[...] = jnp.zeros(l_ref.shape, F32)
        acc_ref[...] = jnp.zeros(acc_ref.shape, F32)
        _softmax_step([_dot_nt(qe, k_ref[...])], [v_ref[...]], m_ref, l_ref, acc_ref)
        outs.append(acc_ref[...] / l_ref[...])
    o_ref[...] = _merge_pair(*outs).astype(BF16)


def _pair_attn_call(q, k, v):
    n = q.shape[0]
    spec = lambda rows: pl.BlockSpec((rows, LANES), lambda hp: (0, hp))
    return pl.pallas_call(
        _pair_attn_kernel,
        grid=(NA_HEADS // 2,),
        in_specs=[spec(n), spec(k.shape[0]), spec(v.shape[0])],
        out_specs=spec(n),
        out_shape=jax.ShapeDtypeStruct((n, NA_WIDTH), BF16),
        scratch_shapes=[pltpu.VMEM((n, 1), F32), pltpu.VMEM((n, 1), F32),
                        pltpu.VMEM((n, LANES), F32)],
        compiler_params=_params(1),
        name="na_ctx",
    )(q, k, v)


def _na_bias_tables(rpb, rows):
    w, kw, kh = GRID_W, NA_KW, NA_KH
    c = np.arange(w)
    c0 = np.clip(c - kw // 2, 0, w - kw)
    kc = np.arange(w)
    col_valid = (kc[None, :] >= c0[:, None]) & (kc[None, :] < c0[:, None] + kw)
    col_idx = np.clip(kc[None, :] - c[:, None] + (kw - 1), 0, 2 * kw - 2)
    by_col = jnp.where(col_valid[None, None], rpb[:, :, col_idx], NEG)

    q_row0 = np.array([0, NA_QROWS, rows - NA_QROWS])
    k_row0 = np.clip(q_row0 - kh // 2, 0, rows - NA_KROWS)
    r = q_row0[:, None, None] + np.arange(NA_QROWS)[None, :, None]
    kr = k_row0[:, None, None] + np.arange(NA_KROWS)[None, None, :]
    r0 = np.clip(r - kh // 2, 0, rows - kh)
    row_valid = (kr >= r0) & (kr < r0 + kh)
    row_idx = np.clip(kr - r + (kh - 1), 0, 2 * kh - 2)
    t = by_col[:, row_idx]
    t = jnp.where(row_valid[None, :, :, :, None, None], t, NEG)
    h = rpb.shape[0]
    return t.transpose(1, 0, 2, 4, 3, 5).reshape(3, h, NA_QROWS * w, NA_KROWS * w)


def _att_out_kernel(mla_ref, na_ref, sg_ref, x_ref, mod_ref, w_ref, o_ref, *, row):
    d = x_ref.shape[1]
    a = (mla_ref[...].astype(F32) * sg_ref[:, 0:MLA_WIDTH].astype(F32)).astype(BF16)
    b = (na_ref[...].astype(F32) * sg_ref[:, MLA_WIDTH:].astype(F32)).astype(BF16)
    y = _dot(a, w_ref[0:MLA_WIDTH, :]) + _dot(b, w_ref[MLA_WIDTH:, :])
    o_ref[...] = x_ref[...] + mod_ref[row:row + 1, 2 * d:3 * d] * y


def _att_out_call(mla, na, sg, x, mod, w_out, *, row, tm):
    n, d = x.shape
    tok = lambda width: pl.BlockSpec((tm, width), lambda i: (i, 0))
    return pl.pallas_call(
        functools.partial(_att_out_kernel, row=row),
        grid=(n // tm,),
        in_specs=[tok(MLA_WIDTH), tok(NA_WIDTH), tok(MLA_WIDTH + NA_WIDTH), tok(d),
                  _const_spec(mod.shape), _const_spec(w_out.shape)],
        out_specs=tok(d),
        out_shape=jax.ShapeDtypeStruct((n, d), F32),
        compiler_params=_params(1, VMEM_LIMIT),
        name="att_out_ctx" if row else "att_out",
    )(mla, na, sg, x, mod, w_out)


def _sgu_kernel(x_ref, mod_ref, ng_ref, win_ref, lng_ref, lnb_ref, ws_ref, bs_ref, wout_ref, fn_ref,
                o_ref, *, row, final):
    tm, d = x_ref.shape
    x = x_ref[...]
    shift = mod_ref[row:row + 1, 0:d]
    scale = mod_ref[row:row + 1, d:2 * d]
    gate = mod_ref[row:row + 1, 2 * d:3 * d]
    hb = (_rms(x, ng_ref[...]) * (1.0 + scale) + shift).astype(BF16)

    v = _gelu(_dot(hb, win_ref[:, SGU_WIDTH:2 * SGU_WIDTH]))
    mu = jnp.mean(v, axis=-1, keepdims=True)
    vc = v - mu
    var = jnp.mean(vc * vc, axis=-1, keepdims=True)
    vn = (vc * lax.rsqrt(var + EPS) * lng_ref[...] + lnb_ref[...]).astype(BF16)

    y = jnp.zeros((tm, d), F32)
    for g in range(SGU_GROUPS):
        cols = slice(g * SGU_GROUP_DIM, (g + 1) * SGU_GROUP_DIM)
        u = _gelu(_dot(hb, win_ref[:, g * SGU_GROUP_DIM:(g + 1) * SGU_GROUP_DIM]))
        gt = _silu(_dot(hb, win_ref[:, 2 * SGU_WIDTH + g * SGU_GROUP_DIM:
                                    2 * SGU_WIDTH + (g + 1) * SGU_GROUP_DIM]))
        sv = jnp.concatenate(
            [_dot(ws_ref[g], vn[c * SGU_CHUNK:(c + 1) * SGU_CHUNK, cols]) + bs_ref[g]
             for c in range(tm // SGU_CHUNK)], axis=0)
        y = y + _dot((u * sv * gt).astype(BF16), wout_ref[cols, :])
    out = x + gate * y
    if final:
        out = _rms(out, fn_ref[...])
    o_ref[...] = out


def _sgu_call(x, mod, ng, w, final_norm, *, row, tm, final):
    n, d = x.shape
    tok = pl.BlockSpec((tm, d), lambda i: (i, 0))
    consts = [mod, ng, w["w_in"], w["ln_g"], w["ln_b"], w["w_s"], w["b_s"], w["w_out"], final_norm]
    return pl.pallas_call(
        functools.partial(_sgu_kernel, row=row, final=final),
        grid=(n // tm,),
        in_specs=[tok] + [_const_spec(a.shape) for a in consts],
        out_specs=tok,
        out_shape=jax.ShapeDtypeStruct((n, d), F32),
        compiler_params=_params(1, VMEM_LIMIT),
        name="sgu_ctx" if row else "sgu",
    )(x, *consts)


def _pack_att_weights(w_in, q_norm, w_uq, kv_norm, w_ukv, w_out):
    d = w_in.shape[0]
    bounds = np.cumsum([0, MLA_Q_LORA, MLA_KV_LORA, MLA_ROPE, MLA_WIDTH, NA_WIDTH, NA_WIDTH, NA_WIDTH, NA_WIDTH])
    cq, ckv, kr, gm, qn, kn, vn, gn = [w_in[:, a:b] for a, b in zip(bounds[:-1], bounds[1:])]
    half = MLA_ROPE // 2
    x1, x2 = kr[:, :half], kr[:, half:]
    kr_group = jnp.concatenate([jnp.zeros((d, MLA_NOPE), F32), x1, x2, x2, x1], axis=1)
    na_scale = NA_HEAD_DIM ** -0.5
    packed = jnp.concatenate([cq, ckv, kr_group, gm, qn * na_scale, kn, vn, gn], axis=1)
    assert packed.shape[1] == _C_END

    uq = w_uq.reshape(MLA_Q_LORA, MLA_HEADS, MLA_NOPE + MLA_ROPE)
    q1, q2 = uq[..., MLA_NOPE:MLA_NOPE + half], uq[..., MLA_NOPE + half:]
    uq = jnp.concatenate([uq[..., :MLA_NOPE], q1, q2, q2, q1], axis=-1).reshape(MLA_Q_LORA, MLA_HEADS * LANES)

    ukv = w_ukv.reshape(MLA_KV_LORA, MLA_HEADS, MLA_NOPE + 64)
    wk = jnp.concatenate([ukv[..., :MLA_NOPE], jnp.zeros_like(ukv[..., :LANES - MLA_NOPE])], axis=-1)
    wk = wk.reshape(MLA_KV_LORA, MLA_HEADS * LANES)
    wv = ukv[..., MLA_NOPE:].reshape(MLA_KV_LORA, MLA_WIDTH)
    return {"w_in": packed.astype(BF16), "q_norm": q_norm.reshape(1, -1), "w_uq": uq.astype(BF16),
            "kv_norm": kv_norm.reshape(1, -1), "w_k": wk.astype(BF16), "w_v": wv.astype(BF16),
            "w_out": w_out.astype(BF16)}


def _rope_tables(s, lc):
    t = jnp.arange(s)
    row = (t // GRID_W).astype(F32)
    col = (t % GRID_W).astype(F32)
    axis_dims = MLA_ROPE // 2
    inv = jnp.power(ROPE_BASE, -jnp.arange(0, axis_dims, 2, dtype=F32) / axis_dims)
    ang = jnp.concatenate([row[:, None] * inv, col[:, None] * inv], axis=-1)
    cos, sin = jnp.cos(ang), jnp.sin(ang)
    scale = (MLA_NOPE + MLA_ROPE) ** -0.5 * math.log2(math.e)
    z = lambda n, w: jnp.zeros((n, w), F32)
    o = lambda n, w: jnp.ones((n, w), F32)
    pad = LANES - MLA_NOPE - MLA_ROPE
    cos_q = jnp.concatenate([o(s, MLA_NOPE), cos, cos, z(s, pad)], axis=1) * scale
    sin_q = jnp.concatenate([z(s, MLA_NOPE), -sin, sin, z(s, pad)], axis=1) * scale
    cos_k = jnp.concatenate([z(s, MLA_NOPE), cos, cos, z(s, pad)], axis=1)
    sin_k = jnp.concatenate([z(s, MLA_NOPE), -sin, sin, z(s, pad)], axis=1)
    ctx_q = jnp.concatenate([o(lc, MLA_NOPE + MLA_ROPE), z(lc, pad)], axis=1) * scale
    ctx_k = jnp.concatenate([z(lc, MLA_NOPE), o(lc, MLA_ROPE), z(lc, pad)], axis=1)
    return (cos_q, sin_q, cos_k, sin_k), (ctx_q, z(lc, LANES), ctx_k, z(lc, LANES))


def kernel(x, c, ctx, c_ctx, norm_g, w_mod, b_mod, att_w_in, mla_q_norm, mla_w_uq, mla_kv_norm, mla_w_ukv,
           na_rpb, att_w_out, sgu_w_in, sgu_ln_g, sgu_ln_b, sgu_w_s, sgu_b_s, sgu_w_out, final_norm):
    batch, s, d = x.shape
    lc = ctx.shape[1]
    depth = norm_g.shape[0]
    assert batch == 1 and c.shape[0] == 1
    rows = s // GRID_W
    assert s % (NA_QROWS * GRID_W) == 0 and rows >= NA_KROWS

    tm = 512 if s % 512 == 0 else 256
    tq = 512 if s % 512 == 0 else 256
    tk = 512

    cond = jnp.concatenate([c, c_ctx[None, :], jnp.zeros((6, d), F32)], axis=0)
    mods = _mod_call(cond, w_mod, b_mod)
    tabs_x, tabs_c = _rope_tables(s, lc)
    fnorm = final_norm.reshape(1, d)

    xs, xc = x[0], ctx[0]
    last_ctx_reader = max(l for l in range(depth) if l % 2 == 0)
    for l in range(depth):
        i = l // 2
        update_ctx = l < last_ctx_reader
        ng = norm_g[l].reshape(1, d)
        if l % 2 == 0:
            w = _pack_att_weights(att_w_in[i], mla_q_norm[i], mla_w_uq[i], mla_kv_norm[i], mla_w_ukv[i],
                                  att_w_out[i])
            bias = _na_bias_tables(na_rpb[i], rows)
            qx, kx, vx, qnx, knx, vnx, sgx = _att_in_call(xs, mods[l], ng, w, tabs_x, row=0, tm=tm)
            qc, kc, vc, qnc, knc, vnc, sgc = _att_in_call(xc, mods[l], ng, w, tabs_c, row=1, tm=lc)
            mla_x = _mla_call(qx, kx, vx, kc, vc, tq=tq, tk=tk)
            na_x = _na_call(qnx, knx, vnx, knc, vnc, bias)
            xs = _att_out_call(mla_x, na_x, sgx, xs, mods[l], w["w_out"], row=0, tm=tm)
            if update_ctx:
                mla_c = _mla_call(qc, None, None, kc, vc, tq=lc, tk=tk)
                na_c = _pair_attn_call(qnc, knc, vnc)
                xc = _att_out_call(mla_c, na_c, sgc, xc, mods[l], w["w_out"], row=1, tm=lc)
        else:
            w = {"w_in": sgu_w_in[i].astype(BF16), "ln_g": sgu_ln_g[i].reshape(1, -1),
                 "ln_b": sgu_ln_b[i].reshape(1, -1), "w_s": sgu_w_s[i].astype(BF16),
                 "b_s": sgu_b_s[i][:, :, None], "w_out": sgu_w_out[i].astype(BF16)}
            final = l == depth - 1
            xs = _sgu_call(xs, mods[l], ng, w, fnorm, row=0, tm=tm, final=final)
            if update_ctx:
                xc = _sgu_call(xc, mods[l], ng, w, fnorm, row=1, tm=lc, final=False)
    if depth % 2 == 1:
        raise NotImplementedError("final RMSNorm is fused into a trailing spatial-gating layer")
    return xs[None]
```

```python
import functools
import math

import numpy as np
import jax
import jax.numpy as jnp
from jax import lax
from jax.experimental import pallas as pl
from jax.experimental.pallas import tpu as pltpu

F32 = jnp.float32
BF16 = jnp.bfloat16

EPS = 1e-6
NEG = -1e30
LANES = 128
VMEM_LIMIT = 56 * 1024 * 1024

GRID_W = 64
MLA_HEADS = 8
MLA_NOPE = 64
MLA_ROPE = 32
MLA_V = 64
SUM_ROWS = 16
MLA_Q_LORA = 768
MLA_KV_LORA = 256
MLA_WIDTH = 512
ROPE_BASE = 10000.0
NA_HEADS = 8
NA_HEAD_DIM = 64
NA_WIDTH = 512
NA_KH = 8
NA_KW = 16
SGU_CHUNK = 128
SGU_WIDTH = 2048
SGU_GROUPS = 8
SGU_GROUP_DIM = SGU_WIDTH // SGU_GROUPS

NA_QROWS = 8
NA_KROWS = 16


def _dot(a, b):
    return jnp.dot(a, b, preferred_element_type=F32)


def _dot_nt(a, b):
    return lax.dot_general(a, b, (((1,), (1,)), ((), ())), preferred_element_type=F32)


def _rms(x, g):
    return x * lax.rsqrt(jnp.mean(x * x, axis=-1, keepdims=True) + EPS) * g


def _silu(x):
    return x * jax.nn.sigmoid(x)


def _gelu(x):
    return 0.5 * x * (1.0 + lax.erf(x * np.float32(math.sqrt(0.5))))


def _params(n_axes, vmem=None):
    return pltpu.CompilerParams(dimension_semantics=("arbitrary",) * n_axes,
                                vmem_limit_bytes=vmem)


def _const_spec(shape):
    zeros = (0,) * len(shape)
    return pl.BlockSpec(shape, lambda *_: zeros, pipeline_mode=pl.Buffered(1))


def _mod_kernel(cond_ref, w_ref, b_ref, o_ref):
    s = _silu(cond_ref[...])
    w = w_ref[0]
    s_hi = s.astype(BF16)
    s_lo = (s - s_hi.astype(F32)).astype(BF16)
    w_hi = w.astype(BF16)
    w_lo = (w - w_hi.astype(F32)).astype(BF16)
    o_ref[0] = _dot(s_hi, w_hi) + _dot(s_hi, w_lo) + _dot(s_lo, w_hi) + b_ref[0]


def _mod_call(cond, w_mod, b_mod):
    depth, d, d3 = w_mod.shape
    tn = 1024
    return pl.pallas_call(
        _mod_kernel,
        grid=(depth, d3 // tn),
        in_specs=[pl.BlockSpec((8, d), lambda l, j: (0, 0)),
                  pl.BlockSpec((1, d, tn), lambda l, j: (l, 0, j)),
                  pl.BlockSpec((1, 1, tn), lambda l, j: (l, 0, j))],
        out_specs=pl.BlockSpec((1, 8, tn), lambda l, j: (l, 0, j)),
        out_shape=jax.ShapeDtypeStruct((depth, 8, d3), F32),
        compiler_params=_params(2),
        name="adaln_mod",
    )(cond, w_mod, b_mod.reshape(depth, 1, d3))


_C_CQ = 0
_C_CKV = _C_CQ + MLA_Q_LORA
_C_KR = _C_CKV + MLA_KV_LORA
_C_GM = _C_KR + LANES
_C_QN = _C_GM + MLA_WIDTH
_C_KN = _C_QN + NA_WIDTH
_C_VN = _C_KN + NA_WIDTH
_C_GN = _C_VN + NA_WIDTH
_C_END = _C_GN + NA_WIDTH


def _att_in_kernel(x_ref, mod_ref, ng_ref, win_ref, qnorm_ref, wuq_ref, kvnorm_ref, wk_ref, wv_ref,
                   cq_ref, sq_ref, ck_ref, sk_ref,
                   qt_ref, k_ref, vt_ref, qn_ref, kn_ref, vn_ref, sg_ref, *, row):
    d = x_ref.shape[1]
    shift = mod_ref[row:row + 1, 0:d]
    scale = mod_ref[row:row + 1, d:2 * d]
    hb = (_rms(x_ref[...], ng_ref[...]) * (1.0 + scale) + shift).astype(BF16)

    def proj(lo, hi):
        return _dot(hb, win_ref[:, lo:hi])

    cqn = _rms(proj(_C_CQ, _C_CKV), qnorm_ref[...]).astype(BF16)
    q = _dot(cqn, wuq_ref[...])
    q_sw = pltpu.roll(q, q.shape[1] - MLA_ROPE, 1)
    cq_t, sq_t = cq_ref[...], sq_ref[...]
    for h in range(MLA_HEADS):
        sl = slice(h * LANES, (h + 1) * LANES)
        qt_ref[sl, :] = (q[:, sl] * cq_t + q_sw[:, sl] * sq_t).T.astype(BF16)

    ckvn = _rms(proj(_C_CKV, _C_KR), kvnorm_ref[...]).astype(BF16)
    k_nope = _dot(ckvn, wk_ref[...])
    kr = proj(_C_KR, _C_GM)
    kr_rot = kr * ck_ref[...] + pltpu.roll(kr, LANES - MLA_ROPE, 1) * sk_ref[...]
    for h in range(MLA_HEADS):
        sl = slice(h * LANES, (h + 1) * LANES)
        k_ref[:, sl] = (k_nope[:, sl] + kr_rot).astype(BF16)
    vt_ref[...] = _dot(ckvn, wv_ref[...]).T.astype(BF16)

    qn_ref[...] = proj(_C_QN, _C_KN).astype(BF16)
    kn_ref[...] = proj(_C_KN, _C_VN).astype(BF16)
    vn_ref[...] = proj(_C_VN, _C_GN).astype(BF16)
    sg_ref[:, 0:MLA_WIDTH] = _silu(proj(_C_GM, _C_QN)).astype(BF16)
    sg_ref[:, MLA_WIDTH:] = _silu(proj(_C_GN, _C_END)).astype(BF16)


def _att_in_call(x, mod, ng, w, tabs, *, row, tm):
    n, d = x.shape
    hw = MLA_HEADS * LANES
    tok = lambda width: pl.BlockSpec((tm, width), lambda i: (i, 0))
    tok_t = lambda width: pl.BlockSpec((width, tm), lambda i: (0, i))
    out_widths = (hw, hw, MLA_WIDTH, NA_WIDTH, NA_WIDTH, NA_WIDTH, MLA_WIDTH + NA_WIDTH)
    transposed = (True, False, True, False, False, False, False)
    return pl.pallas_call(
        functools.partial(_att_in_kernel, row=row),
        grid=(n // tm,),
        in_specs=[tok(d), _const_spec(mod.shape), _const_spec(ng.shape),
                  _const_spec(w["w_in"].shape), _const_spec(w["q_norm"].shape),
                  _const_spec(w["w_uq"].shape), _const_spec(w["kv_norm"].shape),
                  _const_spec(w["w_k"].shape), _const_spec(w["w_v"].shape),
                  tok(LANES), tok(LANES), tok(LANES), tok(LANES)],
        out_specs=[tok_t(wd) if t else tok(wd) for wd, t in zip(out_widths, transposed)],
        out_shape=[jax.ShapeDtypeStruct((wd, n) if t else (n, wd), BF16)
                   for wd, t in zip(out_widths, transposed)],
        compiler_params=_params(1, VMEM_LIMIT),
        name="att_in_ctx" if row else "att_in",
    )(x, mod, ng, w["w_in"], w["q_norm"], w["w_uq"], w["kv_norm"], w["w_k"], w["w_v"], *tabs)


def _softmax_step(s_list, v_list, m_ref, l_ref, acc_ref):
    m_old = m_ref[...]
    m_new = m_old
    for s in s_list:
        m_new = jnp.maximum(m_new, jnp.max(s, axis=1, keepdims=True))
    alpha = jnp.exp(m_old - m_new)
    l_new = alpha * l_ref[...]
    acc = alpha * acc_ref[...]
    for s, v in zip(s_list, v_list):
        p = jnp.exp(s - m_new)
        l_new = l_new + jnp.sum(p, axis=1, keepdims=True)
        acc = acc + _dot(p.astype(BF16), v)
    m_ref[...] = m_new
    l_ref[...] = l_new
    acc_ref[...] = acc


def _merge_pair(o_even, o_odd):
    lane = lax.broadcasted_iota(jnp.int32, o_even.shape, 1)
    return jnp.where(lane < NA_HEAD_DIM, o_even, o_odd)


def _mla_kernel(*refs, tk, n_chunks):
    if n_chunks:
        qt_ref, kx_ref, vxt_ref, kc_ref, vct_ref, o_ref, acc_ref, sa_ref, sb_ref = refs
    else:
        qt_ref, kc_ref, vct_ref, o_ref, acc_ref = refs
    tq = qt_ref.shape[1]
    outs = []
    for e in range(2):
        hs = slice(e * LANES, (e + 1) * LANES)
        vs = slice(e * MLA_V, (e + 1) * MLA_V)
        qt = qt_ref[hs, :]

        def scores(j):
            return _dot(kx_ref[pl.ds(pl.multiple_of(j * tk, tk), tk), hs], qt)

        def values(j):
            return vxt_ref[vs, pl.ds(pl.multiple_of(j * tk, tk), tk)]

        def update(s, vt, m):
            m_new = jnp.maximum(m, jnp.max(s, axis=0, keepdims=True))
            alpha = jnp.exp2(m - m_new)
            p = jnp.exp2(s - m_new).astype(BF16)
            ones_row = (lax.broadcasted_iota(jnp.int32, (SUM_ROWS, vt.shape[1]), 0) == 0).astype(BF16)
            acc_ref[...] = alpha * acc_ref[...] + _dot(jnp.concatenate([vt, ones_row], axis=0), p)
            return m_new

        acc_ref[...] = jnp.zeros(acc_ref.shape, F32)
        m = jnp.full((1, tq), NEG, F32)
        if n_chunks:
            sa_ref[...] = scores(0)

            def body(t, m):
                sb_ref[...] = scores(2 * t + 1)
                m = update(sa_ref[...], values(2 * t), m)
                sa_ref[...] = scores(2 * t + 2)
                return update(sb_ref[...], values(2 * t + 1), m)
            m = lax.fori_loop(0, n_chunks // 2 - 1, body, m)
            sb_ref[...] = scores(n_chunks - 1)
            m = update(sa_ref[...], values(n_chunks - 2), m)
            s_ctx = _dot(kc_ref[:, hs], qt)
            m = update(sb_ref[...], values(n_chunks - 1), m)
        else:
            s_ctx = _dot(kc_ref[:, hs], qt)
        update(s_ctx, vct_ref[vs, :], m)
        outs.append(acc_ref[0:MLA_V, :] / acc_ref[MLA_V:MLA_V + 1, :])
    o_ref[...] = jnp.concatenate(outs, axis=0).T.astype(BF16)


def _mla_call(qt, kx, vxt, kc, vct, *, tq, tk):
    n = qt.shape[1]
    lc = kc.shape[0]
    pairs = MLA_HEADS // 2
    in_specs = [pl.BlockSpec((2 * LANES, tq), lambda hp, i: (hp, i))]
    args = [qt]
    n_chunks = 0
    if kx is not None:
        s = kx.shape[0]
        n_chunks = s // tk
        assert s % tk == 0 and n_chunks % 2 == 0
        in_specs += [pl.BlockSpec((s, 2 * LANES), lambda hp, i: (0, hp)),
                     pl.BlockSpec((LANES, s), lambda hp, i: (hp, 0))]
        args += [kx, vxt]
    in_specs += [pl.BlockSpec((lc, 2 * LANES), lambda hp, i: (0, hp)),
                 pl.BlockSpec((LANES, lc), lambda hp, i: (hp, 0))]
    args += [kc, vct]
    return pl.pallas_call(
        functools.partial(_mla_kernel, tk=tk, n_chunks=n_chunks),
        grid=(pairs, n // tq),
        in_specs=in_specs,
        out_specs=pl.BlockSpec((tq, LANES), lambda hp, i: (i, hp)),
        out_shape=jax.ShapeDtypeStruct((n, MLA_WIDTH), BF16),
        scratch_shapes=[pltpu.VMEM((MLA_V + SUM_ROWS, tq), F32)] + [pltpu.VMEM((tk, tq), F32)] * (2 if n_chunks else 0),
        compiler_params=_params(2, VMEM_LIMIT),
        name="mla_flash" if kx is not None else "mla_ctx",
    )(*args)


def _pair_masks(shape):
    lane = lax.broadcasted_iota(jnp.int32, shape, 1)
    return lane < NA_HEAD_DIM, lane >= NA_HEAD_DIM


def _na_kernel(q_ref, kn_ref, vn_ref, kc_ref, vc_ref, bias_ref, o_ref, m_ref, l_ref, acc_ref, *, n_tok):
    tq = q_ref.shape[0]
    tkw = NA_KROWS * GRID_W
    b = pl.program_id(1)
    start = jnp.clip(b * tq - (NA_KH // 2) * GRID_W, 0, n_tok - tkw)
    start = pl.multiple_of(start, (NA_KH // 2) * GRID_W)
    k_win = kn_ref[pl.ds(start, tkw), :]
    v_win = vn_ref[pl.ds(start, tkw), :]
    q = q_ref[...]
    outs = []
    for e, mask in enumerate(_pair_masks(q.shape)):
        qe = jnp.where(mask, q, jnp.zeros_like(q))
        m_ref[...] = jnp.full(m_ref.shape, NEG, F32)
        l_ref[...] = jnp.zeros(l_ref.shape, F32)
        acc_ref[...] = jnp.zeros(acc_ref.shape, F32)
        s_win = _dot_nt(qe, k_win) + bias_ref[0, e]
        s_ctx = _dot_nt(qe, kc_ref[...])
        _softmax_step([s_win, s_ctx], [v_win, vc_ref[...]], m_ref, l_ref, acc_ref)
        outs.append(acc_ref[...] / l_ref[...])
    o_ref[...] = _merge_pair(*outs).astype(BF16)


def _na_call(qn, kn, vn, knc, vnc, bias):
    n = qn.shape[0]
    lc = knc.shape[0]
    tq = NA_QROWS * GRID_W
    tkw = NA_KROWS * GRID_W
    nb = n // tq
    pairs = NA_HEADS // 2

    def bias_map(hp, b):
        return (jnp.where(b == 0, 0, jnp.where(b == nb - 1, 2, 1)), hp, 0, 0)

    return pl.pallas_call(
        functools.partial(_na_kernel, n_tok=n),
        grid=(pairs, nb),
        in_specs=[pl.BlockSpec((tq, LANES), lambda hp, b: (b, hp)),
                  pl.BlockSpec((n, LANES), lambda hp, b: (0, hp)),
                  pl.BlockSpec((n, LANES), lambda hp, b: (0, hp)),
                  pl.BlockSpec((lc, LANES), lambda hp, b: (0, hp)),
                  pl.BlockSpec((lc, LANES), lambda hp, b: (0, hp)),
                  pl.BlockSpec((1, 2, tq, tkw), bias_map)],
        out_specs=pl.BlockSpec((tq, LANES), lambda hp, b: (b, hp)),
        out_shape=jax.ShapeDtypeStruct((n, NA_WIDTH), BF16),
        scratch_shapes=[pltpu.VMEM((tq, 1), F32), pltpu.VMEM((tq, 1), F32),
                        pltpu.VMEM((tq, LANES), F32)],
        compiler_params=_params(2, VMEM_LIMIT),
        name="na_window",
    )(qn, kn, vn, knc, vnc, bias)


def _pair_attn_kernel(q_ref, k_ref, v_ref, o_ref, m_ref, l_ref, acc_ref):
    q = q_ref[...]
    outs = []
    for mask in _pair_masks(q.shape):
        qe = jnp.where(mask, q, jnp.zeros_like(q))
        m_ref[...] = jnp.full(m_ref.shape, NEG, F32)
        l_ref[...] = jnp.zeros(l_ref.shape, F32)
        acc_ref[...] = jnp.zeros(acc_ref.shape, F32)
        _softmax_step([_dot_nt(qe, k_ref[...])], [v_ref[...]], m_ref, l_ref, acc_ref)
        outs.append(acc_ref[...] / l_ref[...])
    o_ref[...] = _merge_pair(*outs).astype(BF16)


def _pair_attn_call(q, k, v):
    n = q.shape[0]
    spec = lambda rows: pl.BlockSpec((rows, LANES), lambda hp: (0, hp))
    return pl.pallas_call(
        _pair_attn_kernel,
        grid=(NA_HEADS // 2,),
        in_specs=[spec(n), spec(k.shape[0]), spec(v.shape[0])],
        out_specs=spec(n),
        out_shape=jax.ShapeDtypeStruct((n, NA_WIDTH), BF16),
        scratch_shapes=[pltpu.VMEM((n, 1), F32), pltpu.VMEM((n, 1), F32),
                        pltpu.VMEM((n, LANES), F32)],
        compiler_params=_params(1),
        name="na_ctx",
    )(q, k, v)


def _na_bias_tables(rpb, rows):
    w, kw, kh = GRID_W, NA_KW, NA_KH
    c = np.arange(w)
    c0 = np.clip(c - kw // 2, 0, w - kw)
    kc = np.arange(w)
    col_valid = (kc[None, :] >= c0[:, None]) & (kc[None, :] < c0[:, None] + kw)
    col_idx = np.clip(kc[None, :] - c[:, None] + (kw - 1), 0, 2 * kw - 2)
    by_col = jnp.where(col_valid[None, None], rpb[:, :, col_idx], NEG)

    q_row0 = np.array([0, NA_QROWS, rows - NA_QROWS])
    k_row0 = np.clip(q_row0 - kh // 2, 0, rows - NA_KROWS)
    r = q_row0[:, None, None] + np.arange(NA_QROWS)[None, :, None]
    kr = k_row0[:, None, None] + np.arange(NA_KROWS)[None, None, :]
    r0 = np.clip(r - kh // 2, 0, rows - kh)
    row_valid = (kr >= r0) & (kr < r0 + kh)
    row_idx = np.clip(kr - r + (kh - 1), 0, 2 * kh - 2)
    t = by_col[:, row_idx]
    t = jnp.where(row_valid[None, :, :, :, None, None], t, NEG)
    h = rpb.shape[0]
    return t.transpose(1, 0, 2, 4, 3, 5).reshape(3, h, NA_QROWS * w, NA_KROWS * w)


def _att_out_kernel(mla_ref, na_ref, sg_ref, x_ref, mod_ref, w_ref, o_ref, *, row):
    d = x_ref.shape[1]
    a = (mla_ref[...].astype(F32) * sg_ref[:, 0:MLA_WIDTH].astype(F32)).astype(BF16)
    b = (na_ref[...].astype(F32) * sg_ref[:, MLA_WIDTH:].astype(F32)).astype(BF16)
    y = _dot(a, w_ref[0:MLA_WIDTH, :]) + _dot(b, w_ref[MLA_WIDTH:, :])
    o_ref[...] = x_ref[...] + mod_ref[row:row + 1, 2 * d:3 * d] * y


def _att_out_call(mla, na, sg, x, mod, w_out, *, row, tm):
    n, d = x.shape
    tok = lambda width: pl.BlockSpec((tm, width), lambda i: (i, 0))
    return pl.pallas_call(
        functools.partial(_att_out_kernel, row=row),
        grid=(n // tm,),
        in_specs=[tok(MLA_WIDTH), tok(NA_WIDTH), tok(MLA_WIDTH + NA_WIDTH), tok(d),
                  _const_spec(mod.shape), _const_spec(w_out.shape)],
        out_specs=tok(d),
        out_shape=jax.ShapeDtypeStruct((n, d), F32),
        compiler_params=_params(1, VMEM_LIMIT),
        name="att_out_ctx" if row else "att_out",
    )(mla, na, sg, x, mod, w_out)


def _sgu_kernel(x_ref, mod_ref, ng_ref, win_ref, lng_ref, lnb_ref, ws_ref, bs_ref, wout_ref, fn_ref,
                o_ref, *, row, final):
    tm, d = x_ref.shape
    x = x_ref[...]
    shift = mod_ref[row:row + 1, 0:d]
    scale = mod_ref[row:row + 1, d:2 * d]
    gate = mod_ref[row:row + 1, 2 * d:3 * d]
    hb = (_rms(x, ng_ref[...]) * (1.0 + scale) + shift).astype(BF16)

    v = _gelu(_dot(hb, win_ref[:, SGU_WIDTH:2 * SGU_WIDTH]))
    mu = jnp.mean(v, axis=-1, keepdims=True)
    vc = v - mu
    var = jnp.mean(vc * vc, axis=-1, keepdims=True)
    vn = (vc * lax.rsqrt(var + EPS) * lng_ref[...] + lnb_ref[...]).astype(BF16)

    y = jnp.zeros((tm, d), F32)
    for g in range(SGU_GROUPS):
        cols = slice(g * SGU_GROUP_DIM, (g + 1) * SGU_GROUP_DIM)
        u = _gelu(_dot(hb, win_ref[:, g * SGU_GROUP_DIM:(g + 1) * SGU_GROUP_DIM]))
        gt = _silu(_dot(hb, win_ref[:, 2 * SGU_WIDTH + g * SGU_GROUP_DIM:
                                    2 * SGU_WIDTH + (g + 1) * SGU_GROUP_DIM]))
        sv = jnp.concatenate(
            [_dot(ws_ref[g], vn[c * SGU_CHUNK:(c + 1) * SGU_CHUNK, cols]) + bs_ref[g]
             for c in range(tm // SGU_CHUNK)], axis=0)
        y = y + _dot((u * sv * gt).astype(BF16), wout_ref[cols, :])
    out = x + gate * y
    if final:
        out = _rms(out, fn_ref[...])
    o_ref[...] = out


def _sgu_call(x, mod, ng, w, final_norm, *, row, tm, final):
    n, d = x.shape
    tok = pl.BlockSpec((tm, d), lambda i: (i, 0))
    consts = [mod, ng, w["w_in"], w["ln_g"], w["ln_b"], w["w_s"], w["b_s"], w["w_out"], final_norm]
    return pl.pallas_call(
        functools.partial(_sgu_kernel, row=row, final=final),
        grid=(n // tm,),
        in_specs=[tok] + [_const_spec(a.shape) for a in consts],
        out_specs=tok,
        out_shape=jax.ShapeDtypeStruct((n, d), F32),
        compiler_params=_params(1, VMEM_LIMIT),
        name="sgu_ctx" if row else "sgu",
    )(x, *consts)


def _pack_att_weights(w_in, q_norm, w_uq, kv_norm, w_ukv, w_out):
    d = w_in.shape[0]
    bounds = np.cumsum([0, MLA_Q_LORA, MLA_KV_LORA, MLA_ROPE, MLA_WIDTH, NA_WIDTH, NA_WIDTH, NA_WIDTH, NA_WIDTH])
    cq, ckv, kr, gm, qn, kn, vn, gn = [w_in[:, a:b] for a, b in zip(bounds[:-1], bounds[1:])]
    half = MLA_ROPE // 2
    x1, x2 = kr[:, :half], kr[:, half:]
    kr_group = jnp.concatenate([jnp.zeros((d, MLA_NOPE), F32), x1, x2, x2, x1], axis=1)
    na_scale = NA_HEAD_DIM ** -0.5
    packed = jnp.concatenate([cq, ckv, kr_group, gm, qn * na_scale, kn, vn, gn], axis=1)
    assert packed.shape[1] == _C_END

    uq = w_uq.reshape(MLA_Q_LORA, MLA_HEADS, MLA_NOPE + MLA_ROPE)
    q1, q2 = uq[..., MLA_NOPE:MLA_NOPE + half], uq[..., MLA_NOPE + half:]
    uq = jnp.concatenate([uq[..., :MLA_NOPE], q1, q2, q2, q1], axis=-1).reshape(MLA_Q_LORA, MLA_HEADS * LANES)

    ukv = w_ukv.reshape(MLA_KV_LORA, MLA_HEADS, MLA_NOPE + 64)
    wk = jnp.concatenate([ukv[..., :MLA_NOPE], jnp.zeros_like(ukv[..., :LANES - MLA_NOPE])], axis=-1)
    wk = wk.reshape(MLA_KV_LORA, MLA_HEADS * LANES)
    wv = ukv[..., MLA_NOPE:].reshape(MLA_KV_LORA, MLA_WIDTH)
    return {"w_in": packed.astype(BF16), "q_norm": q_norm.reshape(1, -1), "w_uq": uq.astype(BF16),
            "kv_norm": kv_norm.reshape(1, -1), "w_k": wk.astype(BF16), "w_v": wv.astype(BF16),
            "w_out": w_out.astype(BF16)}


def _rope_tables(s, lc):
    t = jnp.arange(s)
    row = (t // GRID_W).astype(F32)
    col = (t % GRID_W).astype(F32)
    axis_dims = MLA_ROPE // 2
    inv = jnp.power(ROPE_BASE, -jnp.arange(0, axis_dims, 2, dtype=F32) / axis_dims)
    ang = jnp.concatenate([row[:, None] * inv, col[:, None] * inv], axis=-1)
    cos, sin = jnp.cos(ang), jnp.sin(ang)
    scale = (MLA_NOPE + MLA_ROPE) ** -0.5 * math.log2(math.e)
    z = lambda n, w: jnp.zeros((n, w), F32)
    o = lambda n, w: jnp.ones((n, w), F32)
    pad = LANES - MLA_NOPE - MLA_ROPE
    cos_q = jnp.concatenate([o(s, MLA_NOPE), cos, cos, z(s, pad)], axis=1) * scale
    sin_q = jnp.concatenate([z(s, MLA_NOPE), -sin, sin, z(s, pad)], axis=1) * scale
    cos_k = jnp.concatenate([z(s, MLA_NOPE), cos, cos, z(s, pad)], axis=1)
    sin_k = jnp.concatenate([z(s, MLA_NOPE), -sin, sin, z(s, pad)], axis=1)
    ctx_q = jnp.concatenate([o(lc, MLA_NOPE + MLA_ROPE), z(lc, pad)], axis=1) * scale
    ctx_k = jnp.concatenate([z(lc, MLA_NOPE), o(lc, MLA_ROPE), z(lc, pad)], axis=1)
    return (cos_q, sin_q, cos_k, sin_k), (ctx_q, z(lc, LANES), ctx_k, z(lc, LANES))


def kernel(x, c, ctx, c_ctx, norm_g, w_mod, b_mod, att_w_in, mla_q_norm, mla_w_uq, mla_kv_norm, mla_w_ukv,
           na_rpb, att_w_out, sgu_w_in, sgu_ln_g, sgu_ln_b, sgu_w_s, sgu_b_s, sgu_w_out, final_norm):
    batch, s, d = x.shape
    lc = ctx.shape[1]
    depth = norm_g.shape[0]
    assert batch == 1 and c.shape[0] == 1
    rows = s // GRID_W
    assert s % (NA_QROWS * GRID_W) == 0 and rows >= NA_KROWS

    tm = 512 if s % 512 == 0 else 256
    tq = 512 if s % 512 == 0 else 256
    tk = 1024 if s % 2048 == 0 else 512

    cond = jnp.concatenate([c, c_ctx[None, :], jnp.zeros((6, d), F32)], axis=0)
    mods = _mod_call(cond, w_mod, b_mod)
    tabs_x, tabs_c = _rope_tables(s, lc)
    fnorm = final_norm.reshape(1, d)

    xs, xc = x[0], ctx[0]
    last_ctx_reader = max(l for l in range(depth) if l % 2 == 0)
    for l in range(depth):
        i = l // 2
        update_ctx = l < last_ctx_reader
        ng = norm_g[l].reshape(1, d)
        if l % 2 == 0:
            w = _pack_att_weights(att_w_in[i], mla_q_norm[i], mla_w_uq[i], mla_kv_norm[i], mla_w_ukv[i],
                                  att_w_out[i])
            bias = _na_bias_tables(na_rpb[i], rows)
            qx, kx, vx, qnx, knx, vnx, sgx = _att_in_call(xs, mods[l], ng, w, tabs_x, row=0, tm=tm)
            qc, kc, vc, qnc, knc, vnc, sgc = _att_in_call(xc, mods[l], ng, w, tabs_c, row=1, tm=lc)
            mla_x = _mla_call(qx, kx, vx, kc, vc, tq=tq, tk=tk)
            na_x = _na_call(qnx, knx, vnx, knc, vnc, bias)
            xs = _att_out_call(mla_x, na_x, sgx, xs, mods[l], w["w_out"], row=0, tm=tm)
            if update_ctx:
                mla_c = _mla_call(qc, None, None, kc, vc, tq=lc, tk=tk)
                na_c = _pair_attn_call(qnc, knc, vnc)
                xc = _att_out_call(mla_c, na_c, sgc, xc, mods[l], w["w_out"], row=1, tm=lc)
        else:
            w = {"w_in": sgu_w_in[i].astype(BF16), "ln_g": sgu_ln_g[i].reshape(1, -1),
                 "ln_b": sgu_ln_b[i].reshape(1, -1), "w_s": sgu_w_s[i].astype(BF16),
                 "b_s": sgu_b_s[i][:, :, None], "w_out": sgu_w_out[i].astype(BF16)}
            final = l == depth - 1
            xs = _sgu_call(xs, mods[l], ng, w, fnorm, row=0, tm=tm, final=final)
            if update_ctx:
                xc = _sgu_call(xc, mods[l], ng, w, fnorm, row=1, tm=lc, final=False)
    if depth % 2 == 1:
        raise NotImplementedError("final RMSNorm is fused into a trailing spatial-gating layer")
    return xs[None]
```

```python
import functools
import math

import numpy as np
import jax
import jax.numpy as jnp
from jax import lax
from jax.experimental import pallas as pl
from jax.experimental.pallas import tpu as pltpu

F32 = jnp.float32
BF16 = jnp.bfloat16

EPS = 1e-6
NEG = -1e30
LOG2E = math.log2(math.e)
LANES = 128
VMEM_LIMIT = 56 * 1024 * 1024

GRID_W = 64
MLA_HEADS = 8
MLA_NOPE = 64
MLA_ROPE = 32
MLA_V = 64
SUM_ROWS = 16
MLA_Q_LORA = 768
MLA_KV_LORA = 256
MLA_WIDTH = 512
ROPE_BASE = 10000.0
NA_HEADS = 8
NA_HEAD_DIM = 64
NA_WIDTH = 512
NA_KH = 8
NA_KW = 16
SGU_CHUNK = 128
SGU_WIDTH = 2048
SGU_GROUPS = 8
SGU_GROUP_DIM = SGU_WIDTH // SGU_GROUPS

NA_QROWS = 4
NA_KROWS = 12
NA_TQ = NA_QROWS * GRID_W
NA_SLABS = 16


def _dot(a, b):
    return jnp.dot(a, b, preferred_element_type=F32)


def _rms(x, g):
    return x * lax.rsqrt(jnp.mean(x * x, axis=-1, keepdims=True) + EPS) * g


def _silu(x):
    return x * jax.nn.sigmoid(x)


def _gelu(x):
    return 0.5 * x * (1.0 + lax.erf(x * np.float32(math.sqrt(0.5))))


def _params(n_axes, vmem=None):
    return pltpu.CompilerParams(dimension_semantics=("arbitrary",) * n_axes,
                                vmem_limit_bytes=vmem)


def _const_spec(shape):
    zeros = (0,) * len(shape)
    return pl.BlockSpec(shape, lambda *_: zeros, pipeline_mode=pl.Buffered(1))


def _mod_kernel(cond_ref, w_ref, b_ref, o_ref):
    s = _silu(cond_ref[...])
    w = w_ref[0]
    s_hi = s.astype(BF16)
    s_lo = (s - s_hi.astype(F32)).astype(BF16)
    w_hi = w.astype(BF16)
    w_lo = (w - w_hi.astype(F32)).astype(BF16)
    o_ref[0] = _dot(s_hi, w_hi) + _dot(s_hi, w_lo) + _dot(s_lo, w_hi) + b_ref[0]


def _mod_call(cond, w_mod, b_mod):
    depth, d, d3 = w_mod.shape
    tn = 1024
    return pl.pallas_call(
        _mod_kernel,
        grid=(depth, d3 // tn),
        in_specs=[pl.BlockSpec((8, d), lambda l, j: (0, 0)),
                  pl.BlockSpec((1, d, tn), lambda l, j: (l, 0, j)),
                  pl.BlockSpec((1, 1, tn), lambda l, j: (l, 0, j))],
        out_specs=pl.BlockSpec((1, 8, tn), lambda l, j: (l, 0, j)),
        out_shape=jax.ShapeDtypeStruct((depth, 8, d3), F32),
        compiler_params=_params(2),
        name="adaln_mod",
    )(cond, w_mod, b_mod.reshape(depth, 1, d3))


_C_CQ = 0
_C_CKV = _C_CQ + MLA_Q_LORA
_C_KR = _C_CKV + MLA_KV_LORA
_C_GM = _C_KR + LANES
_C_QN = _C_GM + MLA_WIDTH
_C_KN = _C_QN + NA_WIDTH
_C_VN = _C_KN + NA_WIDTH
_C_GN = _C_VN + NA_WIDTH
_C_END = _C_GN + NA_WIDTH


def _att_in_kernel(x_ref, mod_ref, ng_ref, win_ref, qnorm_ref, wuq_ref, kvnorm_ref, wk_ref, wv_ref,
                   cq_ref, sq_ref, ck_ref, sk_ref,
                   qt_ref, k_ref, vt_ref, qnt_ref, kn_ref, vnt_ref, sg_ref, *, row):
    d = x_ref.shape[1]
    shift = mod_ref[row:row + 1, 0:d]
    scale = mod_ref[row:row + 1, d:2 * d]
    hb = (_rms(x_ref[...], ng_ref[...]) * (1.0 + scale) + shift).astype(BF16)

    def proj(lo, hi):
        return _dot(hb, win_ref[:, lo:hi])

    cqn = _rms(proj(_C_CQ, _C_CKV), qnorm_ref[...]).astype(BF16)
    q = _dot(cqn, wuq_ref[...])
    q_sw = pltpu.roll(q, q.shape[1] - MLA_ROPE, 1)
    cq_t, sq_t = cq_ref[...], sq_ref[...]
    for h in range(MLA_HEADS):
        sl = slice(h * LANES, (h + 1) * LANES)
        qt_ref[sl, :] = (q[:, sl] * cq_t + q_sw[:, sl] * sq_t).T.astype(BF16)

    ckvn = _rms(proj(_C_CKV, _C_KR), kvnorm_ref[...]).astype(BF16)
    k_nope = _dot(ckvn, wk_ref[...])
    kr = proj(_C_KR, _C_GM)
    kr_rot = kr * ck_ref[...] + pltpu.roll(kr, LANES - MLA_ROPE, 1) * sk_ref[...]
    for h in range(MLA_HEADS):
        sl = slice(h * LANES, (h + 1) * LANES)
        k_ref[:, sl] = (k_nope[:, sl] + kr_rot).astype(BF16)
    vt_ref[...] = _dot(ckvn, wv_ref[...]).T.astype(BF16)

    qnt_ref[...] = (proj(_C_QN, _C_KN) * LOG2E).T.astype(BF16)
    kn_ref[...] = proj(_C_KN, _C_VN).astype(BF16)
    vnt_ref[...] = proj(_C_VN, _C_GN).T.astype(BF16)
    sg_ref[:, 0:MLA_WIDTH] = _silu(proj(_C_GM, _C_QN)).astype(BF16)
    sg_ref[:, MLA_WIDTH:] = _silu(proj(_C_GN, _C_END)).astype(BF16)


def _att_in_call(x, mod, ng, w, tabs, *, row, tm):
    n, d = x.shape
    hw = MLA_HEADS * LANES
    tok = lambda width: pl.BlockSpec((tm, width), lambda i: (i, 0))
    tok_t = lambda width: pl.BlockSpec((width, tm), lambda i: (0, i))
    out_widths = (hw, hw, MLA_WIDTH, NA_WIDTH, NA_WIDTH, NA_WIDTH, MLA_WIDTH + NA_WIDTH)
    transposed = (True, False, True, True, False, True, False)
    return pl.pallas_call(
        functools.partial(_att_in_kernel, row=row),
        grid=(n // tm,),
        in_specs=[tok(d), _const_spec(mod.shape), _const_spec(ng.shape),
                  _const_spec(w["w_in"].shape), _const_spec(w["q_norm"].shape),
                  _const_spec(w["w_uq"].shape), _const_spec(w["kv_norm"].shape),
                  _const_spec(w["w_k"].shape), _const_spec(w["w_v"].shape),
                  tok(LANES), tok(LANES), tok(LANES), tok(LANES)],
        out_specs=[tok_t(wd) if t else tok(wd) for wd, t in zip(out_widths, transposed)],
        out_shape=[jax.ShapeDtypeStruct((wd, n) if t else (n, wd), BF16)
                   for wd, t in zip(out_widths, transposed)],
        compiler_params=_params(1, VMEM_LIMIT),
        name="att_in_ctx" if row else "att_in",
    )(x, mod, ng, w["w_in"], w["q_norm"], w["w_uq"], w["kv_norm"], w["w_k"], w["w_v"], *tabs)


def _mla_kernel(*refs, tk, n_chunks):
    heads = (0, 1)
    if n_chunks:
        qt_ref, kx_ref, vxt_ref, kc_ref, vct_ref, o_ref = refs[:6]
        acc_refs, s_refs = refs[6:8], refs[8:12]
    else:
        qt_ref, kc_ref, vct_ref, o_ref = refs[:4]
        acc_refs = refs[4:6]
    tq = qt_ref.shape[1]
    hs = [slice(e * LANES, (e + 1) * LANES) for e in heads]
    vs = [slice(e * MLA_V, (e + 1) * MLA_V) for e in heads]

    def scores(e, j):
        return _dot(kx_ref[pl.ds(pl.multiple_of(j * tk, tk), tk), hs[e]], qt_ref[hs[e], :])

    def ctx_scores(e):
        return _dot(kc_ref[:, hs[e]], qt_ref[hs[e], :])

    def values(e, j):
        return vxt_ref[vs[e], pl.ds(pl.multiple_of(j * tk, tk), tk)]

    def update(e, s, vt, m):
        m_new = jnp.maximum(m, jnp.max(s, axis=0, keepdims=True))
        alpha = jnp.exp2(m - m_new)
        p = jnp.exp2(s - m_new).astype(BF16)
        ones_row = (lax.broadcasted_iota(jnp.int32, (SUM_ROWS, vt.shape[1]), 0) == 0).astype(BF16)
        acc = acc_refs[e]
        acc[...] = alpha * acc[...] + _dot(jnp.concatenate([vt, ones_row], axis=0), p)
        return m_new

    for e in heads:
        acc_refs[e][...] = jnp.zeros(acc_refs[e].shape, F32)
    ms = [jnp.full((1, tq), NEG, F32) for _ in heads]
    if n_chunks:
        for e in heads:
            s_refs[2 * e][...] = scores(e, 0)

        def body(t, ms):
            ms = list(ms)
            for half in (0, 1):
                for e in heads:
                    s_refs[2 * e + 1 - half][...] = scores(e, 2 * t + half + 1)
                    ms[e] = update(e, s_refs[2 * e + half][...], values(e, 2 * t + half), ms[e])
            return tuple(ms)
        ms = list(lax.fori_loop(0, n_chunks // 2 - 1, body, tuple(ms)))
        for e in heads:
            s_refs[2 * e + 1][...] = scores(e, n_chunks - 1)
            ms[e] = update(e, s_refs[2 * e][...], values(e, n_chunks - 2), ms[e])
        s_ctx = []
        for e in heads:
            s_ctx.append(ctx_scores(e))
            ms[e] = update(e, s_refs[2 * e + 1][...], values(e, n_chunks - 1), ms[e])
    else:
        s_ctx = [ctx_scores(e) for e in heads]
    for e in heads:
        update(e, s_ctx[e], vct_ref[vs[e], :], ms[e])
    outs = [acc_refs[e][0:MLA_V, :] / acc_refs[e][MLA_V:MLA_V + 1, :] for e in heads]
    o_ref[...] = jnp.concatenate(outs, axis=0).T.astype(BF16)


def _mla_call(qt, kx, vxt, kc, vct, *, tq, tk):
    n = qt.shape[1]
    lc = kc.shape[0]
    pairs = MLA_HEADS // 2
    in_specs = [pl.BlockSpec((2 * LANES, tq), lambda hp, i: (hp, i))]
    args = [qt]
    n_chunks = 0
    if kx is not None:
        s = kx.shape[0]
        n_chunks = s // tk
        assert s % tk == 0 and n_chunks % 2 == 0
        in_specs += [pl.BlockSpec((s, 2 * LANES), lambda hp, i: (0, hp)),
                     pl.BlockSpec((LANES, s), lambda hp, i: (hp, 0))]
        args += [kx, vxt]
    in_specs += [pl.BlockSpec((lc, 2 * LANES), lambda hp, i: (0, hp)),
                 pl.BlockSpec((LANES, lc), lambda hp, i: (hp, 0))]
    args += [kc, vct]
    return pl.pallas_call(
        functools.partial(_mla_kernel, tk=tk, n_chunks=n_chunks),
        grid=(pairs, n // tq),
        in_specs=in_specs,
        out_specs=pl.BlockSpec((tq, LANES), lambda hp, i: (i, hp)),
        out_shape=jax.ShapeDtypeStruct((n, MLA_WIDTH), BF16),
        scratch_shapes=[pltpu.VMEM((MLA_V + SUM_ROWS, tq), F32)] * 2
        + [pltpu.VMEM((tk, tq), F32)] * (4 if n_chunks else 0),
        compiler_params=_params(2, VMEM_LIMIT),
        name="mla_flash" if kx is not None else "mla_ctx",
    )(*args)


def _head_rows(qt, e):
    row = lax.broadcasted_iota(jnp.int32, qt.shape, 0)
    keep = (row >= e * NA_HEAD_DIM) & (row < (e + 1) * NA_HEAD_DIM)
    return jnp.where(keep, qt, jnp.zeros_like(qt))


def _with_ones_row(vt):
    ones_row = (lax.broadcasted_iota(jnp.int32, (SUM_ROWS, vt.shape[1]), 0) == 0).astype(BF16)
    return jnp.concatenate([vt, ones_row], axis=0)


def _softmax_pv(s_list, vt_list):
    m = s_list[0].max(axis=0, keepdims=True)
    for s in s_list[1:]:
        m = jnp.maximum(m, s.max(axis=0, keepdims=True))
    acc = None
    for s, vt in zip(s_list, vt_list):
        part = _dot(_with_ones_row(vt), jnp.exp2(s - m).astype(BF16))
        acc = part if acc is None else acc + part
    return acc[0:NA_HEAD_DIM, :] / acc[NA_HEAD_DIM:NA_HEAD_DIM + 1, :]


def _na_kernel(qt_ref, kn_ref, vnt_ref, kc_ref, vct_ref, bias_ref, o_ref, *, rows):
    n_sub = qt_ref.shape[1] // NA_TQ
    blk = pl.program_id(1)
    lane_row = lax.broadcasted_iota(jnp.int32, (1, NA_TQ), 1) // GRID_W
    n_win = NA_KROWS * GRID_W

    def geometry(u):
        r = (blk * n_sub + u) * NA_QROWS
        kr0 = jnp.clip(r - NA_KH // 2, 0, rows - NA_KROWS)
        return r, kr0, pl.multiple_of(kr0 * GRID_W, 2 * GRID_W)

    def raw_scores(u, e):
        qt = _head_rows(qt_ref[:, u * NA_TQ:(u + 1) * NA_TQ], e)
        start = geometry(u)[2]
        return _dot(kn_ref[pl.ds(start, n_win), :], qt), _dot(kc_ref[...], qt)

    def finish(u, e, s, s_ctx):
        r, kr0, start = geometry(u)
        win_lo = jnp.clip(r + lane_row - NA_KH // 2, 0, rows - NA_KH)
        tiles = []
        for j in range(NA_KROWS):
            kr = kr0 + j
            in_win = (kr >= win_lo) & (kr < win_lo + NA_KH)
            slabs = [bias_ref[e, jnp.clip(kr - (r + 2 * cp) + NA_KH - 1, 0, NA_SLABS - 1)]
                     for cp in range(NA_QROWS // 2)]
            bias = jnp.concatenate(slabs, axis=1) + jnp.where(in_win, 0.0, NEG)
            tiles.append(s[j * GRID_W:(j + 1) * GRID_W, :] + bias)
        vs = slice(e * NA_HEAD_DIM, (e + 1) * NA_HEAD_DIM)
        return _softmax_pv([jnp.concatenate(tiles, axis=0), s_ctx],
                           [vnt_ref[vs, pl.ds(start, n_win)], vct_ref[vs, :]])

    order = [(u, e) for u in range(n_sub) for e in range(2)]
    outs = {}
    pending = raw_scores(*order[0])
    for i, (u, e) in enumerate(order):
        current = pending
        if i + 1 < len(order):
            pending = raw_scores(*order[i + 1])
        outs[u, e] = finish(u, e, *current)
    o_t = jnp.concatenate([jnp.concatenate([outs[u, e] for u in range(n_sub)], axis=1) for e in range(2)], axis=0)
    o_ref[...] = o_t.T.astype(BF16)


def _na_call(qnt, kn, vnt, knc, vnct, bias, *, rows):
    n = qnt.shape[1]
    lc = knc.shape[0]
    tq = 2 * NA_TQ
    pairs = NA_HEADS // 2
    return pl.pallas_call(
        functools.partial(_na_kernel, rows=rows),
        grid=(pairs, n // tq),
        in_specs=[pl.BlockSpec((LANES, tq), lambda hp, b: (hp, b)),
                  pl.BlockSpec((n, LANES), lambda hp, b: (0, hp)),
                  pl.BlockSpec((LANES, n), lambda hp, b: (hp, 0)),
                  pl.BlockSpec((lc, LANES), lambda hp, b: (0, hp)),
                  pl.BlockSpec((LANES, lc), lambda hp, b: (hp, 0)),
                  pl.BlockSpec((2, NA_SLABS, GRID_W, 2 * GRID_W), lambda hp, b: (hp, 0, 0, 0))],
        out_specs=pl.BlockSpec((tq, LANES), lambda hp, b: (b, hp)),
        out_shape=jax.ShapeDtypeStruct((n, NA_WIDTH), BF16),
        compiler_params=_params(2, VMEM_LIMIT),
        name="na_window",
    )(qnt, kn, vnt, knc, vnct, bias)


def _pair_attn_kernel(qt_ref, k_ref, vt_ref, o_ref):
    outs = []
    for e in range(2):
        vs = slice(e * NA_HEAD_DIM, (e + 1) * NA_HEAD_DIM)
        outs.append(_softmax_pv([_dot(k_ref[...], _head_rows(qt_ref[...], e))], [vt_ref[vs, :]]))
    o_ref[...] = jnp.concatenate(outs, axis=0).T.astype(BF16)


def _pair_attn_call(qt, k, vt):
    n = qt.shape[1]
    fm = lambda cols: pl.BlockSpec((LANES, cols), lambda hp: (hp, 0))
    return pl.pallas_call(
        _pair_attn_kernel,
        grid=(NA_HEADS // 2,),
        in_specs=[fm(n), pl.BlockSpec((k.shape[0], LANES), lambda hp: (0, hp)), fm(vt.shape[1])],
        out_specs=pl.BlockSpec((n, LANES), lambda hp: (0, hp)),
        out_shape=jax.ShapeDtypeStruct((n, NA_WIDTH), BF16),
        compiler_params=_params(1),
        name="na_ctx",
    )(qt, k, vt)


def _na_bias_slabs(rpb):
    w, kw = GRID_W, NA_KW
    c = np.arange(w)[None, :]
    kc = np.arange(w)[:, None]
    c0 = np.clip(c - kw // 2, 0, w - kw)
    col_valid = (kc >= c0) & (kc < c0 + kw)
    col_idx = np.clip(kc - c + (kw - 1), 0, 2 * kw - 2)
    h, n_off, _ = rpb.shape
    by_off = jnp.where(col_valid[None, None], rpb[:, :, col_idx] * LOG2E, NEG)
    fill = jnp.full((h, NA_SLABS - n_off, w, w), NEG, F32)
    by_off = jnp.concatenate([by_off, fill], axis=1)
    prev = jnp.concatenate([jnp.full((h, 1, w, w), NEG, F32), by_off[:, :-1]], axis=1)
    return jnp.concatenate([by_off, prev], axis=-1)


def _att_out_kernel(mla_ref, na_ref, sg_ref, x_ref, mod_ref, w_ref, o_ref, *, row):
    d = x_ref.shape[1]
    a = (mla_ref[...].astype(F32) * sg_ref[:, 0:MLA_WIDTH].astype(F32)).astype(BF16)
    b = (na_ref[...].astype(F32) * sg_ref[:, MLA_WIDTH:].astype(F32)).astype(BF16)
    y = _dot(a, w_ref[0:MLA_WIDTH, :]) + _dot(b, w_ref[MLA_WIDTH:, :])
    o_ref[...] = x_ref[...] + mod_ref[row:row + 1, 2 * d:3 * d] * y


def _att_out_call(mla, na, sg, x, mod, w_out, *, row, tm):
    n, d = x.shape
    tok = lambda width: pl.BlockSpec((tm, width), lambda i: (i, 0))
    return pl.pallas_call(
        functools.partial(_att_out_kernel, row=row),
        grid=(n // tm,),
        in_specs=[tok(MLA_WIDTH), tok(NA_WIDTH), tok(MLA_WIDTH + NA_WIDTH), tok(d),
                  _const_spec(mod.shape), _const_spec(w_out.shape)],
        out_specs=tok(d),
        out_shape=jax.ShapeDtypeStruct((n, d), F32),
        compiler_params=_params(1, VMEM_LIMIT),
        name="att_out_ctx" if row else "att_out",
    )(mla, na, sg, x, mod, w_out)


def _sgu_kernel(x_ref, mod_ref, ng_ref, win_ref, lng_ref, lnb_ref, ws_ref, bs_ref, wout_ref, fn_ref,
                o_ref, *, row, final):
    tm, d = x_ref.shape
    x = x_ref[...]
    shift = mod_ref[row:row + 1, 0:d]
    scale = mod_ref[row:row + 1, d:2 * d]
    gate = mod_ref[row:row + 1, 2 * d:3 * d]
    hb = (_rms(x, ng_ref[...]) * (1.0 + scale) + shift).astype(BF16)

    v = _gelu(_dot(hb, win_ref[:, SGU_WIDTH:2 * SGU_WIDTH]))
    mu = jnp.mean(v, axis=-1, keepdims=True)
    vc = v - mu
    var = jnp.mean(vc * vc, axis=-1, keepdims=True)
    vn = (vc * lax.rsqrt(var + EPS) * lng_ref[...] + lnb_ref[...]).astype(BF16)

    y = jnp.zeros((tm, d), F32)
    for g in range(SGU_GROUPS):
        cols = slice(g * SGU_GROUP_DIM, (g + 1) * SGU_GROUP_DIM)
        u = _gelu(_dot(hb, win_ref[:, g * SGU_GROUP_DIM:(g + 1) * SGU_GROUP_DIM]))
        gt = _silu(_dot(hb, win_ref[:, 2 * SGU_WIDTH + g * SGU_GROUP_DIM:
                                    2 * SGU_WIDTH + (g + 1) * SGU_GROUP_DIM]))
        sv = jnp.concatenate(
            [_dot(ws_ref[g], vn[c * SGU_CHUNK:(c + 1) * SGU_CHUNK, cols]) + bs_ref[g]
             for c in range(tm // SGU_CHUNK)], axis=0)
        y = y + _dot((u * sv * gt).astype(BF16), wout_ref[cols, :])
    out = x + gate * y
    if final:
        out = _rms(out, fn_ref[...])
    o_ref[...] = out


def _sgu_call(x, mod, ng, w, final_norm, *, row, tm, final):
    n, d = x.shape
    tok = pl.BlockSpec((tm, d), lambda i: (i, 0))
    consts = [mod, ng, w["w_in"], w["ln_g"], w["ln_b"], w["w_s"], w["b_s"], w["w_out"], final_norm]
    return pl.pallas_call(
        functools.partial(_sgu_kernel, row=row, final=final),
        grid=(n // tm,),
        in_specs=[tok] + [_const_spec(a.shape) for a in consts],
        out_specs=tok,
        out_shape=jax.ShapeDtypeStruct((n, d), F32),
        compiler_params=_params(1, VMEM_LIMIT),
        name="sgu_ctx" if row else "sgu",
    )(x, *consts)


def _pack_att_weights(w_in, q_norm, w_uq, kv_norm, w_ukv, w_out):
    d = w_in.shape[0]
    bounds = np.cumsum([0, MLA_Q_LORA, MLA_KV_LORA, MLA_ROPE, MLA_WIDTH, NA_WIDTH, NA_WIDTH, NA_WIDTH, NA_WIDTH])
    cq, ckv, kr, gm, qn, kn, vn, gn = [w_in[:, a:b] for a, b in zip(bounds[:-1], bounds[1:])]
    half = MLA_ROPE // 2
    x1, x2 = kr[:, :half], kr[:, half:]
    kr_group = jnp.concatenate([jnp.zeros((d, MLA_NOPE), F32), x1, x2, x2, x1], axis=1)
    na_scale = NA_HEAD_DIM ** -0.5
    packed = jnp.concatenate([cq, ckv, kr_group, gm, qn * na_scale, kn, vn, gn], axis=1)
    assert packed.shape[1] == _C_END

    uq = w_uq.reshape(MLA_Q_LORA, MLA_HEADS, MLA_NOPE + MLA_ROPE)
    q1, q2 = uq[..., MLA_NOPE:MLA_NOPE + half], uq[..., MLA_NOPE + half:]
    uq = jnp.concatenate([uq[..., :MLA_NOPE], q1, q2, q2, q1], axis=-1).reshape(MLA_Q_LORA, MLA_HEADS * LANES)

    ukv = w_ukv.reshape(MLA_KV_LORA, MLA_HEADS, MLA_NOPE + 64)
    wk = jnp.concatenate([ukv[..., :MLA_NOPE], jnp.zeros_like(ukv[..., :LANES - MLA_NOPE])], axis=-1)
    wk = wk.reshape(MLA_KV_LORA, MLA_HEADS * LANES)
    wv = ukv[..., MLA_NOPE:].reshape(MLA_KV_LORA, MLA_WIDTH)
    return {"w_in": packed.astype(BF16), "q_norm": q_norm.reshape(1, -1), "w_uq": uq.astype(BF16),
            "kv_norm": kv_norm.reshape(1, -1), "w_k": wk.astype(BF16), "w_v": wv.astype(BF16),
            "w_out": w_out.astype(BF16)}


def _rope_tables(s, lc):
    t = jnp.arange(s)
    row = (t // GRID_W).astype(F32)
    col = (t % GRID_W).astype(F32)
    axis_dims = MLA_ROPE // 2
    inv = jnp.power(ROPE_BASE, -jnp.arange(0, axis_dims, 2, dtype=F32) / axis_dims)
    ang = jnp.concatenate([row[:, None] * inv, col[:, None] * inv], axis=-1)
    cos, sin = jnp.cos(ang), jnp.sin(ang)
    scale = (MLA_NOPE + MLA_ROPE) ** -0.5 * LOG2E
    z = lambda n, w: jnp.zeros((n, w), F32)
    o = lambda n, w: jnp.ones((n, w), F32)
    pad = LANES - MLA_NOPE - MLA_ROPE
    cos_q = jnp.concatenate([o(s, MLA_NOPE), cos, cos, z(s, pad)], axis=1) * scale
    sin_q = jnp.concatenate([z(s, MLA_NOPE), -sin, sin, z(s, pad)], axis=1) * scale
    cos_k = jnp.concatenate([z(s, MLA_NOPE), cos, cos, z(s, pad)], axis=1)
    sin_k = jnp.concatenate([z(s, MLA_NOPE), -sin, sin, z(s, pad)], axis=1)
    ctx_q = jnp.concatenate([o(lc, MLA_NOPE + MLA_ROPE), z(lc, pad)], axis=1) * scale
    ctx_k = jnp.concatenate([z(lc, MLA_NOPE), o(lc, MLA_ROPE), z(lc, pad)], axis=1)
    return (cos_q, sin_q, cos_k, sin_k), (ctx_q, z(lc, LANES), ctx_k, z(lc, LANES))


def kernel(x, c, ctx, c_ctx, norm_g, w_mod, b_mod, att_w_in, mla_q_norm, mla_w_uq, mla_kv_norm, mla_w_ukv,
           na_rpb, att_w_out, sgu_w_in, sgu_ln_g, sgu_ln_b, sgu_w_s, sgu_b_s, sgu_w_out, final_norm):
    batch, s, d = x.shape
    lc = ctx.shape[1]
    depth = norm_g.shape[0]
    assert batch == 1 and c.shape[0] == 1
    rows = s // GRID_W
    assert s % (2 * NA_TQ) == 0 and rows >= NA_KROWS

    tm = 512 if s % 512 == 0 else 256
    tq = 512 if s % 512 == 0 else 256
    tk = 1024 if s % 2048 == 0 else 512

    cond = jnp.concatenate([c, c_ctx[None, :], jnp.zeros((6, d), F32)], axis=0)
    mods = _mod_call(cond, w_mod, b_mod)
    tabs_x, tabs_c = _rope_tables(s, lc)
    fnorm = final_norm.reshape(1, d)

    xs, xc = x[0], ctx[0]
    last_ctx_reader = max(l for l in range(depth) if l % 2 == 0)
    for l in range(depth):
        i = l // 2
        update_ctx = l < last_ctx_reader
        ng = norm_g[l].reshape(1, d)
        if l % 2 == 0:
            w = _pack_att_weights(att_w_in[i], mla_q_norm[i], mla_w_uq[i], mla_kv_norm[i], mla_w_ukv[i],
                                  att_w_out[i])
            bias = _na_bias_slabs(na_rpb[i])
            qx, kx, vx, qnx, knx, vnx, sgx = _att_in_call(xs, mods[l], ng, w, tabs_x, row=0, tm=tm)
            qc, kc, vc, qnc, knc, vnc, sgc = _att_in_call(xc, mods[l], ng, w, tabs_c, row=1, tm=lc)
            mla_x = _mla_call(qx, kx, vx, kc, vc, tq=tq, tk=tk)
            na_x = _na_call(qnx, knx, vnx, knc, vnc, bias, rows=rows)
            xs = _att_out_call(mla_x, na_x, sgx, xs, mods[l], w["w_out"], row=0, tm=tm)
            if update_ctx:
                mla_c = _mla_call(qc, None, None, kc, vc, tq=lc, tk=tk)
                na_c = _pair_attn_call(qnc, knc, vnc)
                xc = _att_out_call(mla_c, na_c, sgc, xc, mods[l], w["w_out"], row=1, tm=lc)
        else:
            w = {"w_in": sgu_w_in[i].astype(BF16), "ln_g": sgu_ln_g[i].reshape(1, -1),
                 "ln_b": sgu_ln_b[i].reshape(1, -1), "w_s": sgu_w_s[i].astype(BF16),
                 "b_s": sgu_b_s[i][:, :, None], "w_out": sgu_w_out[i].astype(BF16)}
            final = l == depth - 1
            xs = _sgu_call(xs, mods[l], ng, w, fnorm, row=0, tm=tm, final=final)
            if update_ctx:
                xc = _sgu_call(xc, mods[l], ng, w, fnorm, row=1, tm=lc, final=False)
    if depth % 2 == 1:
        raise NotImplementedError("final RMSNorm is fused into a trailing spatial-gating layer")
    return xs[None]
```

```python
import functools
import math

import numpy as np
import jax
import jax.numpy as jnp
from jax import lax
from jax.experimental import pallas as pl
from jax.experimental.pallas import tpu as pltpu

F32 = jnp.float32
BF16 = jnp.bfloat16

EPS = 1e-6
NEG = -1e30
LOG2E = math.log2(math.e)
LANES = 128
VMEM_LIMIT = 56 * 1024 * 1024

GRID_W = 64
MLA_HEADS = 8
MLA_NOPE = 64
MLA_ROPE = 32
MLA_V = 64
SUM_ROWS = 16
MLA_UNROLL = 4
MLA_Q_LORA = 768
MLA_KV_LORA = 256
MLA_WIDTH = 512
ROPE_BASE = 10000.0
NA_HEADS = 8
NA_HEAD_DIM = 64
NA_WIDTH = 512
NA_KH = 8
NA_KW = 16
SGU_CHUNK = 128
SGU_WIDTH = 2048
SGU_GROUPS = 8
SGU_GROUP_DIM = SGU_WIDTH // SGU_GROUPS

NA_QROWS = 4
NA_KROWS = 12
NA_TQ = NA_QROWS * GRID_W
NA_SLABS = 16


def _dot(a, b):
    return jnp.dot(a, b, preferred_element_type=F32)


def _rms(x, g):
    return x * lax.rsqrt(jnp.mean(x * x, axis=-1, keepdims=True) + EPS) * g


def _silu(x):
    return x * jax.nn.sigmoid(x)


def _gelu(x):
    return 0.5 * x * (1.0 + lax.erf(x * np.float32(math.sqrt(0.5))))


def _params(n_axes, vmem=None):
    return pltpu.CompilerParams(dimension_semantics=("arbitrary",) * n_axes,
                                vmem_limit_bytes=vmem)


def _const_spec(shape):
    zeros = (0,) * len(shape)
    return pl.BlockSpec(shape, lambda *_: zeros, pipeline_mode=pl.Buffered(1))


def _mod_kernel(cond_ref, w_ref, b_ref, o_ref):
    s = _silu(cond_ref[...])
    w = w_ref[0]
    s_hi = s.astype(BF16)
    s_lo = (s - s_hi.astype(F32)).astype(BF16)
    w_hi = w.astype(BF16)
    w_lo = (w - w_hi.astype(F32)).astype(BF16)
    o_ref[0] = _dot(s_hi, w_hi) + _dot(s_hi, w_lo) + _dot(s_lo, w_hi) + b_ref[0]


def _mod_call(cond, w_mod, b_mod):
    depth, d, d3 = w_mod.shape
    tn = 1024
    return pl.pallas_call(
        _mod_kernel,
        grid=(depth, d3 // tn),
        in_specs=[pl.BlockSpec((8, d), lambda l, j: (0, 0)),
                  pl.BlockSpec((1, d, tn), lambda l, j: (l, 0, j)),
                  pl.BlockSpec((1, 1, tn), lambda l, j: (l, 0, j))],
        out_specs=pl.BlockSpec((1, 8, tn), lambda l, j: (l, 0, j)),
        out_shape=jax.ShapeDtypeStruct((depth, 8, d3), F32),
        compiler_params=_params(2),
        name="adaln_mod",
    )(cond, w_mod, b_mod.reshape(depth, 1, d3))


_C_CQ = 0
_C_CKV = _C_CQ + MLA_Q_LORA
_C_KR = _C_CKV + MLA_KV_LORA
_C_GM = _C_KR + LANES
_C_QN = _C_GM + MLA_WIDTH
_C_KN = _C_QN + NA_WIDTH
_C_VN = _C_KN + NA_WIDTH
_C_GN = _C_VN + NA_WIDTH
_C_END = _C_GN + NA_WIDTH


def _att_in_kernel(x_ref, mod_ref, ng_ref, win_ref, qnorm_ref, wuq_ref, kvnorm_ref, wk_ref, wv_ref,
                   ck_ref, sk_ref,
                   qt_ref, k_ref, vt_ref, qnt_ref, kn_ref, vnt_ref, sg_ref, *, row):
    d = x_ref.shape[1]
    shift = mod_ref[row:row + 1, 0:d]
    scale = mod_ref[row:row + 1, d:2 * d]
    hb = (_rms(x_ref[...], ng_ref[...]) * (1.0 + scale) + shift).astype(BF16)

    def proj(lo, hi):
        return _dot(hb, win_ref[:, lo:hi])

    cqn = _rms(proj(_C_CQ, _C_CKV), qnorm_ref[...]).astype(BF16)
    q = _dot(cqn, wuq_ref[...])
    q_sw = pltpu.roll(q, q.shape[1] - MLA_ROPE, 1)
    ck_t, sk_t = ck_ref[...], sk_ref[...]
    nope = (lax.broadcasted_iota(jnp.int32, (1, LANES), 1) < MLA_NOPE).astype(F32)
    q_scale = (MLA_NOPE + MLA_ROPE) ** -0.5 * LOG2E
    cq_t, sq_t = (ck_t + nope) * q_scale, sk_t * q_scale
    for h in range(MLA_HEADS):
        sl = slice(h * LANES, (h + 1) * LANES)
        qt_ref[sl, :] = (q[:, sl] * cq_t + q_sw[:, sl] * sq_t).T.astype(BF16)

    ckvn = _rms(proj(_C_CKV, _C_KR), kvnorm_ref[...]).astype(BF16)
    k_nope = _dot(ckvn, wk_ref[...])
    kr = proj(_C_KR, _C_GM)
    kr_rot = kr * ck_t + pltpu.roll(kr, LANES - MLA_ROPE, 1) * sk_t
    for h in range(MLA_HEADS):
        sl = slice(h * LANES, (h + 1) * LANES)
        k_ref[:, sl] = (k_nope[:, sl] + kr_rot).astype(BF16)
    vt_ref[...] = _dot(ckvn, wv_ref[...]).T.astype(BF16)

    qnt_ref[...] = (proj(_C_QN, _C_KN) * LOG2E).T.astype(BF16)
    kn_ref[...] = proj(_C_KN, _C_VN).astype(BF16)
    vnt_ref[...] = proj(_C_VN, _C_GN).T.astype(BF16)
    sg_ref[:, 0:MLA_WIDTH] = _silu(proj(_C_GM, _C_QN)).astype(BF16)
    sg_ref[:, MLA_WIDTH:] = _silu(proj(_C_GN, _C_END)).astype(BF16)


def _att_in_call(x, mod, ng, w, tabs, *, row, tm):
    n, d = x.shape
    hw = MLA_HEADS * LANES
    tok = lambda width: pl.BlockSpec((tm, width), lambda i: (i, 0))
    tok_t = lambda width: pl.BlockSpec((width, tm), lambda i: (0, i))
    out_widths = (hw, hw, MLA_WIDTH, NA_WIDTH, NA_WIDTH, NA_WIDTH, MLA_WIDTH + NA_WIDTH)
    transposed = (True, False, True, True, False, True, False)
    return pl.pallas_call(
        functools.partial(_att_in_kernel, row=row),
        grid=(n // tm,),
        in_specs=[tok(d), _const_spec(mod.shape), _const_spec(ng.shape),
                  _const_spec(w["w_in"].shape), _const_spec(w["q_norm"].shape),
                  _const_spec(w["w_uq"].shape), _const_spec(w["kv_norm"].shape),
                  _const_spec(w["w_k"].shape), _const_spec(w["w_v"].shape),
                  tok(LANES), tok(LANES)],
        out_specs=[tok_t(wd) if t else tok(wd) for wd, t in zip(out_widths, transposed)],
        out_shape=[jax.ShapeDtypeStruct((wd, n) if t else (n, wd), BF16)
                   for wd, t in zip(out_widths, transposed)],
        compiler_params=_params(1, VMEM_LIMIT),
        name="att_in_ctx" if row else "att_in",
    )(x, mod, ng, w["w_in"], w["q_norm"], w["w_uq"], w["kv_norm"], w["w_k"], w["w_v"], *tabs)


def _mla_kernel(*refs, tk, n_chunks):
    heads = (0, 1)
    if n_chunks:
        qt_ref, kx_ref, vxt_ref, kc_ref, vct_ref, o_ref = refs[:6]
        acc_refs, s_refs = refs[6:8], refs[8:12]
    else:
        qt_ref, kc_ref, vct_ref, o_ref = refs[:4]
        acc_refs = refs[4:6]
    tq = qt_ref.shape[1]
    hs = [slice(e * LANES, (e + 1) * LANES) for e in heads]
    vs = [slice(e * MLA_V, (e + 1) * MLA_V) for e in heads]

    def chunk(j):
        return pl.ds(j * tk if isinstance(j, int) else pl.multiple_of(j * tk, tk), tk)

    def scores(e, j):
        return _dot(kx_ref[chunk(j), hs[e]], qt_ref[hs[e], :])

    def ctx_scores(e):
        return _dot(kc_ref[:, hs[e]], qt_ref[hs[e], :])

    def values(e, j):
        return vxt_ref[vs[e], chunk(j)]

    def update(e, s, vt, m):
        m_new = jnp.maximum(m, jnp.max(s, axis=0, keepdims=True))
        alpha = jnp.exp2(m - m_new)
        p = jnp.exp2(s - m_new).astype(BF16)
        ones_row = (lax.broadcasted_iota(jnp.int32, (SUM_ROWS, vt.shape[1]), 0) == 0).astype(BF16)
        acc = acc_refs[e]
        acc[...] = alpha * acc[...] + _dot(jnp.concatenate([vt, ones_row], axis=0), p)
        return m_new

    for e in heads:
        acc_refs[e][...] = jnp.zeros(acc_refs[e].shape, F32)
    ms = [jnp.full((1, tq), NEG, F32) for _ in heads]
    if n_chunks:
        s_ctx = [None, None]

        def step(c, parity, ms, last=False):
            ms = list(ms)
            for e in heads:
                if last:
                    s_ctx[e] = ctx_scores(e)
                else:
                    s_refs[2 * e + 1 - parity][...] = scores(e, c + 1)
                ms[e] = update(e, s_refs[2 * e + parity][...], values(e, c), ms[e])
            return ms

        for e in heads:
            s_refs[2 * e][...] = scores(e, 0)
        trips = (n_chunks - 1) // MLA_UNROLL

        def body(t, ms):
            for i in range(MLA_UNROLL):
                ms = step(MLA_UNROLL * t + i, i % 2, ms)
            return tuple(ms)
        ms = list(lax.fori_loop(0, trips, body, tuple(ms)))
        for c in range(trips * MLA_UNROLL, n_chunks):
            ms = step(c, c % 2, ms, last=c == n_chunks - 1)
    else:
        s_ctx = [ctx_scores(e) for e in heads]
    for e in heads:
        update(e, s_ctx[e], vct_ref[vs[e], :], ms[e])
    outs = [acc_refs[e][0:MLA_V, :] / acc_refs[e][MLA_V:MLA_V + 1, :] for e in heads]
    o_ref[...] = jnp.concatenate(outs, axis=0).T.astype(BF16)


def _mla_call(qt, kx, vxt, kc, vct, *, tq, tk):
    n = qt.shape[1]
    lc = kc.shape[0]
    pairs = MLA_HEADS // 2
    in_specs = [pl.BlockSpec((2 * LANES, tq), lambda hp, i: (hp, i))]
    args = [qt]
    n_chunks = 0
    if kx is not None:
        s = kx.shape[0]
        n_chunks = s // tk
        assert s % tk == 0 and n_chunks % 2 == 0
        in_specs += [pl.BlockSpec((s, 2 * LANES), lambda hp, i: (0, hp)),
                     pl.BlockSpec((LANES, s), lambda hp, i: (hp, 0))]
        args += [kx, vxt]
    in_specs += [pl.BlockSpec((lc, 2 * LANES), lambda hp, i: (0, hp)),
                 pl.BlockSpec((LANES, lc), lambda hp, i: (hp, 0))]
    args += [kc, vct]
    return pl.pallas_call(
        functools.partial(_mla_kernel, tk=tk, n_chunks=n_chunks),
        grid=(pairs, n // tq),
        in_specs=in_specs,
        out_specs=pl.BlockSpec((tq, LANES), lambda hp, i: (i, hp)),
        out_shape=jax.ShapeDtypeStruct((n, MLA_WIDTH), BF16),
        scratch_shapes=[pltpu.VMEM((MLA_V + SUM_ROWS, tq), F32)] * 2
        + [pltpu.VMEM((tk, tq), F32)] * (4 if n_chunks else 0),
        compiler_params=_params(2, VMEM_LIMIT),
        name="mla_flash" if kx is not None else "mla_ctx",
    )(*args)


def _head_rows(qt, e):
    row = lax.broadcasted_iota(jnp.int32, qt.shape, 0)
    keep = (row >= e * NA_HEAD_DIM) & (row < (e + 1) * NA_HEAD_DIM)
    return jnp.where(keep, qt, jnp.zeros_like(qt))


def _with_ones_row(vt):
    ones_row = (lax.broadcasted_iota(jnp.int32, (SUM_ROWS, vt.shape[1]), 0) == 0).astype(BF16)
    return jnp.concatenate([vt, ones_row], axis=0)


def _softmax_pv(s_list, vt_list):
    m = s_list[0].max(axis=0, keepdims=True)
    for s in s_list[1:]:
        m = jnp.maximum(m, s.max(axis=0, keepdims=True))
    acc = None
    for s, vt in zip(s_list, vt_list):
        part = _dot(_with_ones_row(vt), jnp.exp2(s - m).astype(BF16))
        acc = part if acc is None else acc + part
    return acc[0:NA_HEAD_DIM, :] / acc[NA_HEAD_DIM:NA_HEAD_DIM + 1, :]


def _na_kernel(qt_ref, kn_ref, vnt_ref, kc_ref, vct_ref, bias_ref, o_ref, *, rows):
    n_sub = qt_ref.shape[1] // NA_TQ
    blk = pl.program_id(1)
    lane_row = lax.broadcasted_iota(jnp.int32, (1, NA_TQ), 1) // GRID_W
    n_win = NA_KROWS * GRID_W

    def geometry(u):
        r = (blk * n_sub + u) * NA_QROWS
        kr0 = jnp.clip(r - NA_KH // 2, 0, rows - NA_KROWS)
        return r, kr0, pl.multiple_of(kr0 * GRID_W, 2 * GRID_W)

    def raw_scores(u, e):
        qt = _head_rows(qt_ref[:, u * NA_TQ:(u + 1) * NA_TQ], e)
        start = geometry(u)[2]
        return _dot(kn_ref[pl.ds(start, n_win), :], qt), _dot(kc_ref[...], qt)

    def finish(u, e, s, s_ctx):
        r, kr0, start = geometry(u)
        win_lo = jnp.clip(r + lane_row - NA_KH // 2, 0, rows - NA_KH)
        tiles = []
        for j in range(NA_KROWS):
            kr = kr0 + j
            in_win = (kr >= win_lo) & (kr < win_lo + NA_KH)
            slabs = [bias_ref[e, jnp.clip(kr - (r + 2 * cp) + NA_KH - 1, 0, NA_SLABS - 1)]
                     for cp in range(NA_QROWS // 2)]
            bias = jnp.concatenate(slabs, axis=1) + jnp.where(in_win, 0.0, NEG)
            tiles.append(s[j * GRID_W:(j + 1) * GRID_W, :] + bias)
        vs = slice(e * NA_HEAD_DIM, (e + 1) * NA_HEAD_DIM)
        return _softmax_pv([jnp.concatenate(tiles, axis=0), s_ctx],
                           [vnt_ref[vs, pl.ds(start, n_win)], vct_ref[vs, :]])

    order = [(u, e) for u in range(n_sub) for e in range(2)]
    outs = {}
    pending = raw_scores(*order[0])
    for i, (u, e) in enumerate(order):
        current = pending
        if i + 1 < len(order):
            pending = raw_scores(*order[i + 1])
        outs[u, e] = finish(u, e, *current)
    o_t = jnp.concatenate([jnp.concatenate([outs[u, e] for u in range(n_sub)], axis=1) for e in range(2)], axis=0)
    o_ref[...] = o_t.T.astype(BF16)


def _na_call(qnt, kn, vnt, knc, vnct, bias, *, rows):
    n = qnt.shape[1]
    lc = knc.shape[0]
    tq = 2 * NA_TQ
    pairs = NA_HEADS // 2
    return pl.pallas_call(
        functools.partial(_na_kernel, rows=rows),
        grid=(pairs, n // tq),
        in_specs=[pl.BlockSpec((LANES, tq), lambda hp, b: (hp, b)),
                  pl.BlockSpec((n, LANES), lambda hp, b: (0, hp)),
                  pl.BlockSpec((LANES, n), lambda hp, b: (hp, 0)),
                  pl.BlockSpec((lc, LANES), lambda hp, b: (0, hp)),
                  pl.BlockSpec((LANES, lc), lambda hp, b: (hp, 0)),
                  pl.BlockSpec((2, NA_SLABS, GRID_W, 2 * GRID_W), lambda hp, b: (hp, 0, 0, 0))],
        out_specs=pl.BlockSpec((tq, LANES), lambda hp, b: (b, hp)),
        out_shape=jax.ShapeDtypeStruct((n, NA_WIDTH), BF16),
        compiler_params=_params(2, VMEM_LIMIT),
        name="na_window",
    )(qnt, kn, vnt, knc, vnct, bias)


def _pair_attn_kernel(qt_ref, k_ref, vt_ref, o_ref):
    outs = []
    for e in range(2):
        vs = slice(e * NA_HEAD_DIM, (e + 1) * NA_HEAD_DIM)
        outs.append(_softmax_pv([_dot(k_ref[...], _head_rows(qt_ref[...], e))], [vt_ref[vs, :]]))
    o_ref[...] = jnp.concatenate(outs, axis=0).T.astype(BF16)


def _pair_attn_call(qt, k, vt):
    n = qt.shape[1]
    fm = lambda cols: pl.BlockSpec((LANES, cols), lambda hp: (hp, 0))
    return pl.pallas_call(
        _pair_attn_kernel,
        grid=(NA_HEADS // 2,),
        in_specs=[fm(n), pl.BlockSpec((k.shape[0], LANES), lambda hp: (0, hp)), fm(vt.shape[1])],
        out_specs=pl.BlockSpec((n, LANES), lambda hp: (0, hp)),
        out_shape=jax.ShapeDtypeStruct((n, NA_WIDTH), BF16),
        compiler_params=_params(1),
        name="na_ctx",
    )(qt, k, vt)


def _na_bias_slabs(rpb):
    w, kw = GRID_W, NA_KW
    c = np.arange(w)[None, :]
    kc = np.arange(w)[:, None]
    c0 = np.clip(c - kw // 2, 0, w - kw)
    col_valid = (kc >= c0) & (kc < c0 + kw)
    col_idx = np.clip(kc - c + (kw - 1), 0, 2 * kw - 2)
    h, n_off, n_col = rpb.shape
    onehot = (col_idx.reshape(1, -1) == np.arange(n_col)[:, None]).astype(np.float32)
    by_off = jnp.dot(rpb.reshape(h * n_off, n_col) * LOG2E, onehot, precision=lax.Precision.HIGHEST)
    by_off = jnp.where(col_valid[None, None], by_off.reshape(h, n_off, w, w), NEG)
    fill = jnp.full((h, NA_SLABS - n_off, w, w), NEG, F32)
    by_off = jnp.concatenate([by_off, fill], axis=1)
    prev = jnp.concatenate([jnp.full((h, 1, w, w), NEG, F32), by_off[:, :-1]], axis=1)
    return jnp.concatenate([by_off, prev], axis=-1)


def _att_out_kernel(mla_ref, na_ref, sg_ref, x_ref, mod_ref, w_ref, o_ref, *, row):
    d = x_ref.shape[1]
    a = (mla_ref[...].astype(F32) * sg_ref[:, 0:MLA_WIDTH].astype(F32)).astype(BF16)
    b = (na_ref[...].astype(F32) * sg_ref[:, MLA_WIDTH:].astype(F32)).astype(BF16)
    y = _dot(a, w_ref[0:MLA_WIDTH, :]) + _dot(b, w_ref[MLA_WIDTH:, :])
    o_ref[...] = x_ref[...] + mod_ref[row:row + 1, 2 * d:3 * d] * y


def _att_out_call(mla, na, sg, x, mod, w_out, *, row, tm):
    n, d = x.shape
    tok = lambda width: pl.BlockSpec((tm, width), lambda i: (i, 0))
    return pl.pallas_call(
        functools.partial(_att_out_kernel, row=row),
        grid=(n // tm,),
        in_specs=[tok(MLA_WIDTH), tok(NA_WIDTH), tok(MLA_WIDTH + NA_WIDTH), tok(d),
                  _const_spec(mod.shape), _const_spec(w_out.shape)],
        out_specs=tok(d),
        out_shape=jax.ShapeDtypeStruct((n, d), F32),
        compiler_params=_params(1, VMEM_LIMIT),
        name="att_out_ctx" if row else "att_out",
    )(mla, na, sg, x, mod, w_out)


def _sgu_kernel(x_ref, mod_ref, ng_ref, win_ref, lng_ref, lnb_ref, ws_ref, bs_ref, wout_ref, fn_ref,
                o_ref, *, row, final):
    tm, d = x_ref.shape
    x = x_ref[...]
    shift = mod_ref[row:row + 1, 0:d]
    scale = mod_ref[row:row + 1, d:2 * d]
    gate = mod_ref[row:row + 1, 2 * d:3 * d]
    hb = (_rms(x, ng_ref[...]) * (1.0 + scale) + shift).astype(BF16)

    v = _gelu(_dot(hb, win_ref[:, SGU_WIDTH:2 * SGU_WIDTH]))
    mu = jnp.mean(v, axis=-1, keepdims=True)
    vc = v - mu
    var = jnp.mean(vc * vc, axis=-1, keepdims=True)
    vn = (vc * lax.rsqrt(var + EPS) * lng_ref[...] + lnb_ref[...]).astype(BF16)

    y = jnp.zeros((tm, d), F32)
    for g in range(SGU_GROUPS):
        cols = slice(g * SGU_GROUP_DIM, (g + 1) * SGU_GROUP_DIM)
        u = _gelu(_dot(hb, win_ref[:, g * SGU_GROUP_DIM:(g + 1) * SGU_GROUP_DIM]))
        gt = _silu(_dot(hb, win_ref[:, 2 * SGU_WIDTH + g * SGU_GROUP_DIM:
                                    2 * SGU_WIDTH + (g + 1) * SGU_GROUP_DIM]))
        sv = jnp.concatenate(
            [_dot(ws_ref[g], vn[c * SGU_CHUNK:(c + 1) * SGU_CHUNK, cols]) + bs_ref[g]
             for c in range(tm // SGU_CHUNK)], axis=0)
        y = y + _dot((u * sv * gt).astype(BF16), wout_ref[cols, :])
    out = x + gate * y
    if final:
        out = _rms(out, fn_ref[...])
    o_ref[...] = out


def _sgu_call(x, mod, ng, w, final_norm, *, row, tm, final):
    n, d = x.shape
    tok = pl.BlockSpec((tm, d), lambda i: (i, 0))
    consts = [mod, ng, w["w_in"], w["ln_g"], w["ln_b"], w["w_s"], w["b_s"], w["w_out"], final_norm]
    return pl.pallas_call(
        functools.partial(_sgu_kernel, row=row, final=final),
        grid=(n // tm,),
        in_specs=[tok] + [_const_spec(a.shape) for a in consts],
        out_specs=tok,
        out_shape=jax.ShapeDtypeStruct((n, d), F32),
        compiler_params=_params(1, VMEM_LIMIT),
        name="sgu_ctx" if row else "sgu",
    )(x, *consts)


def _pack_att_weights(w_in, q_norm, w_uq, kv_norm, w_ukv, w_out):
    d = w_in.shape[0]
    bounds = np.cumsum([0, MLA_Q_LORA, MLA_KV_LORA, MLA_ROPE, MLA_WIDTH, NA_WIDTH, NA_WIDTH, NA_WIDTH, NA_WIDTH])
    cq, ckv, kr, gm, qn, kn, vn, gn = [w_in[:, a:b] for a, b in zip(bounds[:-1], bounds[1:])]
    half = MLA_ROPE // 2
    x1, x2 = kr[:, :half], kr[:, half:]
    kr_group = jnp.concatenate([jnp.zeros((d, MLA_NOPE), F32), x1, x2, x2, x1], axis=1)
    na_scale = NA_HEAD_DIM ** -0.5
    packed = jnp.concatenate([cq, ckv, kr_group, gm, qn * na_scale, kn, vn, gn], axis=1)
    assert packed.shape[1] == _C_END

    uq = w_uq.reshape(MLA_Q_LORA, MLA_HEADS, MLA_NOPE + MLA_ROPE)
    q1, q2 = uq[..., MLA_NOPE:MLA_NOPE + half], uq[..., MLA_NOPE + half:]
    uq = jnp.concatenate([uq[..., :MLA_NOPE], q1, q2, q2, q1], axis=-1).reshape(MLA_Q_LORA, MLA_HEADS * LANES)

    ukv = w_ukv.reshape(MLA_KV_LORA, MLA_HEADS, MLA_NOPE + 64)
    wk = jnp.concatenate([ukv[..., :MLA_NOPE], jnp.zeros_like(ukv[..., :LANES - MLA_NOPE])], axis=-1)
    wk = wk.reshape(MLA_KV_LORA, MLA_HEADS * LANES)
    wv = ukv[..., MLA_NOPE:].reshape(MLA_KV_LORA, MLA_WIDTH)
    return {"w_in": packed.astype(BF16), "q_norm": q_norm.reshape(1, -1), "w_uq": uq.astype(BF16),
            "kv_norm": kv_norm.reshape(1, -1), "w_k": wk.astype(BF16), "w_v": wv.astype(BF16),
            "w_out": w_out.astype(BF16)}


def _rope_tables(s, lc):
    rows = s // GRID_W
    axis_dims = MLA_ROPE // 2
    inv = jnp.power(ROPE_BASE, -jnp.arange(0, axis_dims, 2, dtype=F32) / axis_dims)
    n_freq = inv.shape[0]
    ang_row = jnp.arange(rows).astype(F32)[:, None] * inv
    ang_col = jnp.arange(GRID_W).astype(F32)[:, None] * inv

    def per_token(f):
        by_row = jnp.broadcast_to(f(ang_row)[:, None, :], (rows, GRID_W, n_freq))
        by_col = jnp.broadcast_to(f(ang_col)[None, :, :], (rows, GRID_W, n_freq))
        return jnp.concatenate([by_row, by_col], axis=-1).reshape(s, 2 * n_freq)

    cos, sin = per_token(jnp.cos), per_token(jnp.sin)
    z = lambda n, w: jnp.zeros((n, w), F32)
    pad = LANES - MLA_NOPE - MLA_ROPE
    cos_k = jnp.concatenate([z(s, MLA_NOPE), cos, cos, z(s, pad)], axis=1)
    sin_k = jnp.concatenate([z(s, MLA_NOPE), -sin, sin, z(s, pad)], axis=1)
    ctx_k = jnp.concatenate([z(lc, MLA_NOPE), jnp.ones((lc, MLA_ROPE), F32), z(lc, pad)], axis=1)
    return (cos_k, sin_k), (ctx_k, z(lc, LANES))


def kernel(x, c, ctx, c_ctx, norm_g, w_mod, b_mod, att_w_in, mla_q_norm, mla_w_uq, mla_kv_norm, mla_w_ukv,
           na_rpb, att_w_out, sgu_w_in, sgu_ln_g, sgu_ln_b, sgu_w_s, sgu_b_s, sgu_w_out, final_norm):
    batch, s, d = x.shape
    lc = ctx.shape[1]
    depth = norm_g.shape[0]
    assert batch == 1 and c.shape[0] == 1
    rows = s // GRID_W
    assert s % (2 * NA_TQ) == 0 and rows >= NA_KROWS

    tm = 512 if s % 512 == 0 else 256
    tq = 512 if s % 512 == 0 else 256
    tk = 1024 if s % 2048 == 0 else 512

    cond = jnp.concatenate([c, c_ctx[None, :], jnp.zeros((6, d), F32)], axis=0)
    mods = _mod_call(cond, w_mod, b_mod)
    tabs_x, tabs_c = _rope_tables(s, lc)
    fnorm = final_norm.reshape(1, d)

    xs, xc = x[0], ctx[0]
    last_ctx_reader = max(l for l in range(depth) if l % 2 == 0)
    for l in range(depth):
        i = l // 2
        update_ctx = l < last_ctx_reader
        ng = norm_g[l].reshape(1, d)
        if l % 2 == 0:
            w = _pack_att_weights(att_w_in[i], mla_q_norm[i], mla_w_uq[i], mla_kv_norm[i], mla_w_ukv[i],
                                  att_w_out[i])
            bias = _na_bias_slabs(na_rpb[i])
            qx, kx, vx, qnx, knx, vnx, sgx = _att_in_call(xs, mods[l], ng, w, tabs_x, row=0, tm=tm)
            qc, kc, vc, qnc, knc, vnc, sgc = _att_in_call(xc, mods[l], ng, w, tabs_c, row=1, tm=lc)
            mla_x = _mla_call(qx, kx, vx, kc, vc, tq=tq, tk=tk)
            na_x = _na_call(qnx, knx, vnx, knc, vnc, bias, rows=rows)
            xs = _att_out_call(mla_x, na_x, sgx, xs, mods[l], w["w_out"], row=0, tm=tm)
            if update_ctx:
                mla_c = _mla_call(qc, None, None, kc, vc, tq=lc, tk=tk)
                na_c = _pair_attn_call(qnc, knc, vnc)
                xc = _att_out_call(mla_c, na_c, sgc, xc, mods[l], w["w_out"], row=1, tm=lc)
        else:
            w = {"w_in": sgu_w_in[i].astype(BF16), "ln_g": sgu_ln_g[i].reshape(1, -1),
                 "ln_b": sgu_ln_b[i].reshape(1, -1), "w_s": sgu_w_s[i].astype(BF16),
                 "b_s": sgu_b_s[i][:, :, None], "w_out": sgu_w_out[i].astype(BF16)}
            final = l == depth - 1
            xs = _sgu_call(xs, mods[l], ng, w, fnorm, row=0, tm=tm, final=final)
            if update_ctx:
                xc = _sgu_call(xc, mods[l], ng, w, fnorm, row=1, tm=lc, final=False)
    if depth % 2 == 1:
        raise NotImplementedError("final RMSNorm is fused into a trailing spatial-gating layer")
    return xs[None]
```

```python
import functools
import math

import numpy as np
import jax
import jax.numpy as jnp
from jax import lax
from jax.experimental import pallas as pl
from jax.experimental.pallas import tpu as pltpu

F32 = jnp.float32
BF16 = jnp.bfloat16

EPS = 1e-6
NEG = -1e30
LOG2E = math.log2(math.e)
LANES = 128
VMEM_LIMIT = 56 * 1024 * 1024

GRID_W = 64
MLA_HEADS = 8
MLA_NOPE = 64
MLA_ROPE = 32
MLA_V = 64
SUM_ROWS = 16
MLA_UNROLL = 4
MLA_Q_LORA = 768
MLA_KV_LORA = 256
MLA_WIDTH = 512
ROPE_BASE = 10000.0
NA_HEADS = 8
NA_HEAD_DIM = 64
NA_WIDTH = 512
NA_KH = 8
NA_KW = 16
SGU_CHUNK = 128
SGU_WIDTH = 2048
SGU_GROUPS = 8
SGU_GROUP_DIM = SGU_WIDTH // SGU_GROUPS

NA_QROWS = 4
NA_KROWS = 12
NA_TQ = NA_QROWS * GRID_W
NA_GROUPS_PER_STEP = 4
NA_SLABS = 16


def _dot(a, b):
    return jnp.dot(a, b, preferred_element_type=F32)


def _rms(x, g):
    return x * lax.rsqrt(jnp.mean(x * x, axis=-1, keepdims=True) + EPS) * g


def _silu(x):
    return x * jax.nn.sigmoid(x)


def _gelu(x):
    return 0.5 * x * (1.0 + lax.erf(x * np.float32(math.sqrt(0.5))))


def _params(n_axes, vmem=None):
    return pltpu.CompilerParams(dimension_semantics=("arbitrary",) * n_axes,
                                vmem_limit_bytes=vmem)


def _const_spec(shape):
    zeros = (0,) * len(shape)
    return pl.BlockSpec(shape, lambda *_: zeros, pipeline_mode=pl.Buffered(1))


def _mod_kernel(cond_ref, w_ref, b_ref, o_ref):
    s = _silu(cond_ref[...])
    w = w_ref[0]
    s_hi = s.astype(BF16)
    s_lo = (s - s_hi.astype(F32)).astype(BF16)
    w_hi = w.astype(BF16)
    w_lo = (w - w_hi.astype(F32)).astype(BF16)
    o_ref[0] = _dot(s_hi, w_hi) + _dot(s_hi, w_lo) + _dot(s_lo, w_hi) + b_ref[0]


def _mod_call(cond, w_mod, b_mod):
    depth, d, d3 = w_mod.shape
    tn = 1024
    return pl.pallas_call(
        _mod_kernel,
        grid=(depth, d3 // tn),
        in_specs=[pl.BlockSpec((8, d), lambda l, j: (0, 0)),
                  pl.BlockSpec((1, d, tn), lambda l, j: (l, 0, j)),
                  pl.BlockSpec((1, 1, tn), lambda l, j: (l, 0, j))],
        out_specs=pl.BlockSpec((1, 8, tn), lambda l, j: (l, 0, j)),
        out_shape=jax.ShapeDtypeStruct((depth, 8, d3), F32),
        compiler_params=_params(2),
        name="adaln_mod",
    )(cond, w_mod, b_mod.reshape(depth, 1, d3))


_C_CQ = 0
_C_CKV = _C_CQ + MLA_Q_LORA
_C_KR = _C_CKV + MLA_KV_LORA
_C_GM = _C_KR + LANES
_C_QN = _C_GM + MLA_WIDTH
_C_KN = _C_QN + NA_WIDTH
_C_VN = _C_KN + NA_WIDTH
_C_GN = _C_VN + NA_WIDTH
_C_END = _C_GN + NA_WIDTH


def _att_in_kernel(x_ref, mod_ref, ng_ref, win_ref, qnorm_ref, wuq_ref, kvnorm_ref, wk_ref, wv_ref,
                   ck_ref, sk_ref,
                   qt_ref, k_ref, vt_ref, qnt_ref, kn_ref, vnt_ref, sg_ref, *, row):
    d = x_ref.shape[1]
    shift = mod_ref[row:row + 1, 0:d]
    scale = mod_ref[row:row + 1, d:2 * d]
    hb = (_rms(x_ref[...], ng_ref[...]) * (1.0 + scale) + shift).astype(BF16)

    def proj(lo, hi):
        return _dot(hb, win_ref[:, lo:hi])

    cqn = _rms(proj(_C_CQ, _C_CKV), qnorm_ref[...]).astype(BF16)
    q = _dot(cqn, wuq_ref[...])
    q_sw = pltpu.roll(q, q.shape[1] - MLA_ROPE, 1)
    ck_t, sk_t = ck_ref[...], sk_ref[...]
    nope = (lax.broadcasted_iota(jnp.int32, (1, LANES), 1) < MLA_NOPE).astype(F32)
    q_scale = (MLA_NOPE + MLA_ROPE) ** -0.5 * LOG2E
    cq_t, sq_t = (ck_t + nope) * q_scale, sk_t * q_scale
    for h in range(MLA_HEADS):
        sl = slice(h * LANES, (h + 1) * LANES)
        qt_ref[sl, :] = (q[:, sl] * cq_t + q_sw[:, sl] * sq_t).T.astype(BF16)

    ckvn = _rms(proj(_C_CKV, _C_KR), kvnorm_ref[...]).astype(BF16)
    k_nope = _dot(ckvn, wk_ref[...])
    kr = proj(_C_KR, _C_GM)
    kr_rot = kr * ck_t + pltpu.roll(kr, LANES - MLA_ROPE, 1) * sk_t
    for h in range(MLA_HEADS):
        sl = slice(h * LANES, (h + 1) * LANES)
        k_ref[:, sl] = (k_nope[:, sl] + kr_rot).astype(BF16)
    vt_ref[...] = _dot(ckvn, wv_ref[...]).T.astype(BF16)

    qnt_ref[...] = (proj(_C_QN, _C_KN) * LOG2E).T.astype(BF16)
    kn_ref[...] = proj(_C_KN, _C_VN).astype(BF16)
    vnt_ref[...] = proj(_C_VN, _C_GN).T.astype(BF16)
    sg_ref[:, 0:MLA_WIDTH] = _silu(proj(_C_GM, _C_QN)).astype(BF16)
    sg_ref[:, MLA_WIDTH:] = _silu(proj(_C_GN, _C_END)).astype(BF16)


def _att_in_call(x, mod, ng, w, tabs, *, row, tm):
    n, d = x.shape
    hw = MLA_HEADS * LANES
    tok = lambda width: pl.BlockSpec((tm, width), lambda i: (i, 0))
    tok_t = lambda width: pl.BlockSpec((width, tm), lambda i: (0, i))
    out_widths = (hw, hw, MLA_WIDTH, NA_WIDTH, NA_WIDTH, NA_WIDTH, MLA_WIDTH + NA_WIDTH)
    transposed = (True, False, True, True, False, True, False)
    return pl.pallas_call(
        functools.partial(_att_in_kernel, row=row),
        grid=(n // tm,),
        in_specs=[tok(d), _const_spec(mod.shape), _const_spec(ng.shape),
                  _const_spec(w["w_in"].shape), _const_spec(w["q_norm"].shape),
                  _const_spec(w["w_uq"].shape), _const_spec(w["kv_norm"].shape),
                  _const_spec(w["w_k"].shape), _const_spec(w["w_v"].shape),
                  tok(LANES), tok(LANES)],
        out_specs=[tok_t(wd) if t else tok(wd) for wd, t in zip(out_widths, transposed)],
        out_shape=[jax.ShapeDtypeStruct((wd, n) if t else (n, wd), BF16)
                   for wd, t in zip(out_widths, transposed)],
        compiler_params=_params(1, VMEM_LIMIT),
        name="att_in_ctx" if row else "att_in",
    )(x, mod, ng, w["w_in"], w["q_norm"], w["w_uq"], w["kv_norm"], w["w_k"], w["w_v"], *tabs)


def _mla_kernel(*refs, tk, n_chunks):
    heads = (0, 1)
    if n_chunks:
        qt_ref, kx_ref, vxt_ref, kc_ref, vct_ref, o_ref = refs[:6]
        acc_refs, s_refs = refs[6:8], refs[8:12]
    else:
        qt_ref, kc_ref, vct_ref, o_ref = refs[:4]
        acc_refs = refs[4:6]
    tq = qt_ref.shape[1]
    hs = [slice(e * LANES, (e + 1) * LANES) for e in heads]
    vs = [slice(e * MLA_V, (e + 1) * MLA_V) for e in heads]

    def chunk(j):
        return pl.ds(j * tk if isinstance(j, int) else pl.multiple_of(j * tk, tk), tk)

    def scores(e, j):
        return _dot(kx_ref[chunk(j), hs[e]], qt_ref[hs[e], :])

    def ctx_scores(e):
        return _dot(kc_ref[:, hs[e]], qt_ref[hs[e], :])

    def values(e, j):
        return vxt_ref[vs[e], chunk(j)]

    def update(e, s, vt, m):
        m_new = jnp.maximum(m, jnp.max(s, axis=0, keepdims=True))
        alpha = jnp.exp2(m - m_new)
        p = jnp.exp2(s - m_new).astype(BF16)
        ones_row = (lax.broadcasted_iota(jnp.int32, (SUM_ROWS, vt.shape[1]), 0) == 0).astype(BF16)
        acc = acc_refs[e]
        acc[...] = alpha * acc[...] + _dot(jnp.concatenate([vt, ones_row], axis=0), p)
        return m_new

    for e in heads:
        acc_refs[e][...] = jnp.zeros(acc_refs[e].shape, F32)
    ms = [jnp.full((1, tq), NEG, F32) for _ in heads]
    if n_chunks:
        s_ctx = [None, None]

        def step(c, parity, ms, last=False):
            ms = list(ms)
            for e in heads:
                if last:
                    s_ctx[e] = ctx_scores(e)
                else:
                    s_refs[2 * e + 1 - parity][...] = scores(e, c + 1)
                ms[e] = update(e, s_refs[2 * e + parity][...], values(e, c), ms[e])
            return ms

        for e in heads:
            s_refs[2 * e][...] = scores(e, 0)
        trips = (n_chunks - 1) // MLA_UNROLL

        def body(t, ms):
            for i in range(MLA_UNROLL):
                ms = step(MLA_UNROLL * t + i, i % 2, ms)
            return tuple(ms)
        ms = list(lax.fori_loop(0, trips, body, tuple(ms)))
        for c in range(trips * MLA_UNROLL, n_chunks):
            ms = step(c, c % 2, ms, last=c == n_chunks - 1)
    else:
        s_ctx = [ctx_scores(e) for e in heads]
    for e in heads:
        update(e, s_ctx[e], vct_ref[vs[e], :], ms[e])
    outs = [acc_refs[e][0:MLA_V, :] / acc_refs[e][MLA_V:MLA_V + 1, :] for e in heads]
    o_ref[...] = jnp.concatenate(outs, axis=0).T.astype(BF16)


def _mla_call(qt, kx, vxt, kc, vct, *, tq, tk):
    n = qt.shape[1]
    lc = kc.shape[0]
    pairs = MLA_HEADS // 2
    in_specs = [pl.BlockSpec((2 * LANES, tq), lambda hp, i: (hp, i))]
    args = [qt]
    n_chunks = 0
    if kx is not None:
        s = kx.shape[0]
        n_chunks = s // tk
        assert s % tk == 0 and n_chunks % 2 == 0
        in_specs += [pl.BlockSpec((s, 2 * LANES), lambda hp, i: (0, hp)),
                     pl.BlockSpec((LANES, s), lambda hp, i: (hp, 0))]
        args += [kx, vxt]
    in_specs += [pl.BlockSpec((lc, 2 * LANES), lambda hp, i: (0, hp)),
                 pl.BlockSpec((LANES, lc), lambda hp, i: (hp, 0))]
    args += [kc, vct]
    return pl.pallas_call(
        functools.partial(_mla_kernel, tk=tk, n_chunks=n_chunks),
        grid=(pairs, n // tq),
        in_specs=in_specs,
        out_specs=pl.BlockSpec((tq, LANES), lambda hp, i: (i, hp)),
        out_shape=jax.ShapeDtypeStruct((n, MLA_WIDTH), BF16),
        scratch_shapes=[pltpu.VMEM((MLA_V + SUM_ROWS, tq), F32)] * 2
        + [pltpu.VMEM((tk, tq), F32)] * (4 if n_chunks else 0),
        compiler_params=_params(2, VMEM_LIMIT),
        name="mla_flash" if kx is not None else "mla_ctx",
    )(*args)


def _head_rows(qt, e):
    row = lax.broadcasted_iota(jnp.int32, qt.shape, 0)
    keep = (row >= e * NA_HEAD_DIM) & (row < (e + 1) * NA_HEAD_DIM)
    return jnp.where(keep, qt, jnp.zeros_like(qt))


def _with_ones_row(vt):
    ones_row = (lax.broadcasted_iota(jnp.int32, (SUM_ROWS, vt.shape[1]), 0) == 0).astype(BF16)
    return jnp.concatenate([vt, ones_row], axis=0)


def _softmax_pv(s_list, vt_list):
    m = s_list[0].max(axis=0, keepdims=True)
    for s in s_list[1:]:
        m = jnp.maximum(m, s.max(axis=0, keepdims=True))
    acc = None
    for s, vt in zip(s_list, vt_list):
        part = _dot(_with_ones_row(vt), jnp.exp2(s - m).astype(BF16))
        acc = part if acc is None else acc + part
    return acc[0:NA_HEAD_DIM, :] / acc[NA_HEAD_DIM:NA_HEAD_DIM + 1, :]


def _na_kernel(qt_ref, kn_ref, vnt_ref, kc_ref, vct_ref, bias_ref, o_ref, *, rows):
    n_sub = qt_ref.shape[1] // NA_TQ
    blk = pl.program_id(1)
    lane_row = lax.broadcasted_iota(jnp.int32, (1, NA_TQ), 1) // GRID_W
    n_win = NA_KROWS * GRID_W

    def geometry(u):
        r = (blk * n_sub + u) * NA_QROWS
        kr0 = jnp.clip(r - NA_KH // 2, 0, rows - NA_KROWS)
        return r, kr0, pl.multiple_of(kr0 * GRID_W, 2 * GRID_W)

    def raw_scores(u, e):
        qt = _head_rows(qt_ref[:, u * NA_TQ:(u + 1) * NA_TQ], e)
        start = geometry(u)[2]
        return _dot(kn_ref[pl.ds(start, n_win), :], qt), _dot(kc_ref[...], qt)

    def finish(u, e, s, s_ctx):
        r, kr0, start = geometry(u)
        win_lo = jnp.clip(r + lane_row - NA_KH // 2, 0, rows - NA_KH)
        tiles = []
        for j in range(NA_KROWS):
            kr = kr0 + j
            in_win = (kr >= win_lo) & (kr < win_lo + NA_KH)
            slabs = [bias_ref[e, jnp.clip(kr - (r + 2 * cp) + NA_KH - 1, 0, NA_SLABS - 1)]
                     for cp in range(NA_QROWS // 2)]
            bias = jnp.concatenate(slabs, axis=1) + jnp.where(in_win, 0.0, NEG)
            tiles.append(s[j * GRID_W:(j + 1) * GRID_W, :] + bias)
        vs = slice(e * NA_HEAD_DIM, (e + 1) * NA_HEAD_DIM)
        return _softmax_pv([jnp.concatenate(tiles, axis=0), s_ctx],
                           [vnt_ref[vs, pl.ds(start, n_win)], vct_ref[vs, :]])

    order = [(u, e) for u in range(n_sub) for e in range(2)]
    outs = {}
    pending = raw_scores(*order[0])
    for i, (u, e) in enumerate(order):
        current = pending
        if i + 1 < len(order):
            pending = raw_scores(*order[i + 1])
        outs[u, e] = finish(u, e, *current)
    o_t = jnp.concatenate([jnp.concatenate([outs[u, e] for u in range(n_sub)], axis=1) for e in range(2)], axis=0)
    o_ref[...] = o_t.T.astype(BF16)


def _na_call(qnt, kn, vnt, knc, vnct, bias, *, rows):
    n = qnt.shape[1]
    lc = knc.shape[0]
    tq = NA_GROUPS_PER_STEP * NA_TQ
    pairs = NA_HEADS // 2
    return pl.pallas_call(
        functools.partial(_na_kernel, rows=rows),
        grid=(pairs, n // tq),
        in_specs=[pl.BlockSpec((LANES, tq), lambda hp, b: (hp, b)),
                  pl.BlockSpec((n, LANES), lambda hp, b: (0, hp)),
                  pl.BlockSpec((LANES, n), lambda hp, b: (hp, 0)),
                  pl.BlockSpec((lc, LANES), lambda hp, b: (0, hp)),
                  pl.BlockSpec((LANES, lc), lambda hp, b: (hp, 0)),
                  pl.BlockSpec((2, NA_SLABS, GRID_W, 2 * GRID_W), lambda hp, b: (hp, 0, 0, 0))],
        out_specs=pl.BlockSpec((tq, LANES), lambda hp, b: (b, hp)),
        out_shape=jax.ShapeDtypeStruct((n, NA_WIDTH), BF16),
        compiler_params=_params(2, VMEM_LIMIT),
        name="na_window",
    )(qnt, kn, vnt, knc, vnct, bias)


def _pair_attn_kernel(qt_ref, k_ref, vt_ref, o_ref):
    outs = []
    for e in range(2):
        vs = slice(e * NA_HEAD_DIM, (e + 1) * NA_HEAD_DIM)
        outs.append(_softmax_pv([_dot(k_ref[...], _head_rows(qt_ref[...], e))], [vt_ref[vs, :]]))
    o_ref[...] = jnp.concatenate(outs, axis=0).T.astype(BF16)


def _pair_attn_call(qt, k, vt):
    n = qt.shape[1]
    fm = lambda cols: pl.BlockSpec((LANES, cols), lambda hp: (hp, 0))
    return pl.pallas_call(
        _pair_attn_kernel,
        grid=(NA_HEADS // 2,),
        in_specs=[fm(n), pl.BlockSpec((k.shape[0], LANES), lambda hp: (0, hp)), fm(vt.shape[1])],
        out_specs=pl.BlockSpec((n, LANES), lambda hp: (0, hp)),
        out_shape=jax.ShapeDtypeStruct((n, NA_WIDTH), BF16),
        compiler_params=_params(1),
        name="na_ctx",
    )(qt, k, vt)


def _na_bias_slabs(rpb):
    w, kw = GRID_W, NA_KW
    c = np.arange(w)[None, :]
    kc = np.arange(w)[:, None]
    c0 = np.clip(c - kw // 2, 0, w - kw)
    col_valid = (kc >= c0) & (kc < c0 + kw)
    col_idx = np.clip(kc - c + (kw - 1), 0, 2 * kw - 2)
    h, n_off, n_col = rpb.shape
    onehot = (col_idx.reshape(1, -1) == np.arange(n_col)[:, None]).astype(np.float32)
    by_off = jnp.dot(rpb.reshape(h * n_off, n_col) * LOG2E, onehot, precision=lax.Precision.HIGHEST)
    by_off = jnp.where(col_valid[None, None], by_off.reshape(h, n_off, w, w), NEG)
    fill = jnp.full((h, NA_SLABS - n_off, w, w), NEG, F32)
    by_off = jnp.concatenate([by_off, fill], axis=1)
    prev = jnp.concatenate([jnp.full((h, 1, w, w), NEG, F32), by_off[:, :-1]], axis=1)
    return jnp.concatenate([by_off, prev], axis=-1)


def _att_out_kernel(mla_ref, na_ref, sg_ref, x_ref, mod_ref, w_ref, o_ref, *, row):
    d = x_ref.shape[1]
    a = (mla_ref[...].astype(F32) * sg_ref[:, 0:MLA_WIDTH].astype(F32)).astype(BF16)
    b = (na_ref[...].astype(F32) * sg_ref[:, MLA_WIDTH:].astype(F32)).astype(BF16)
    y = _dot(a, w_ref[0:MLA_WIDTH, :]) + _dot(b, w_ref[MLA_WIDTH:, :])
    o_ref[...] = x_ref[...] + mod_ref[row:row + 1, 2 * d:3 * d] * y


def _att_out_call(mla, na, sg, x, mod, w_out, *, row, tm):
    n, d = x.shape
    tok = lambda width: pl.BlockSpec((tm, width), lambda i: (i, 0))
    return pl.pallas_call(
        functools.partial(_att_out_kernel, row=row),
        grid=(n // tm,),
        in_specs=[tok(MLA_WIDTH), tok(NA_WIDTH), tok(MLA_WIDTH + NA_WIDTH), tok(d),
                  _const_spec(mod.shape), _const_spec(w_out.shape)],
        out_specs=tok(d),
        out_shape=jax.ShapeDtypeStruct((n, d), F32),
        compiler_params=_params(1, VMEM_LIMIT),
        name="att_out_ctx" if row else "att_out",
    )(mla, na, sg, x, mod, w_out)


def _sgu_kernel(x_ref, mod_ref, ng_ref, win_ref, lng_ref, lnb_ref, ws_ref, bs_ref, wout_ref, fn_ref,
                o_ref, *, row, final):
    tm, d = x_ref.shape
    x = x_ref[...]
    shift = mod_ref[row:row + 1, 0:d]
    scale = mod_ref[row:row + 1, d:2 * d]
    gate = mod_ref[row:row + 1, 2 * d:3 * d]
    hb = (_rms(x, ng_ref[...]) * (1.0 + scale) + shift).astype(BF16)

    v = _gelu(_dot(hb, win_ref[:, SGU_WIDTH:2 * SGU_WIDTH]))
    mu = jnp.mean(v, axis=-1, keepdims=True)
    vc = v - mu
    var = jnp.mean(vc * vc, axis=-1, keepdims=True)
    vn = (vc * lax.rsqrt(var + EPS) * lng_ref[...] + lnb_ref[...]).astype(BF16)

    y = jnp.zeros((tm, d), F32)
    for g in range(SGU_GROUPS):
        cols = slice(g * SGU_GROUP_DIM, (g + 1) * SGU_GROUP_DIM)
        u = _gelu(_dot(hb, win_ref[:, g * SGU_GROUP_DIM:(g + 1) * SGU_GROUP_DIM]))
        gt = _silu(_dot(hb, win_ref[:, 2 * SGU_WIDTH + g * SGU_GROUP_DIM:
                                    2 * SGU_WIDTH + (g + 1) * SGU_GROUP_DIM]))
        sv = jnp.concatenate(
            [_dot(ws_ref[g], vn[c * SGU_CHUNK:(c + 1) * SGU_CHUNK, cols]) + bs_ref[g]
             for c in range(tm // SGU_CHUNK)], axis=0)
        y = y + _dot((u * sv * gt).astype(BF16), wout_ref[cols, :])
    out = x + gate * y
    if final:
        out = _rms(out, fn_ref[...])
    o_ref[...] = out


def _sgu_call(x, mod, ng, w, final_norm, *, row, tm, final):
    n, d = x.shape
    tok = pl.BlockSpec((tm, d), lambda i: (i, 0))
    consts = [mod, ng, w["w_in"], w["ln_g"], w["ln_b"], w["w_s"], w["b_s"], w["w_out"], final_norm]
    return pl.pallas_call(
        functools.partial(_sgu_kernel, row=row, final=final),
        grid=(n // tm,),
        in_specs=[tok] + [_const_spec(a.shape) for a in consts],
        out_specs=tok,
        out_shape=jax.ShapeDtypeStruct((n, d), F32),
        compiler_params=_params(1, VMEM_LIMIT),
        name="sgu_ctx" if row else "sgu",
    )(x, *consts)


def _pack_att_weights(w_in, q_norm, w_uq, kv_norm, w_ukv, w_out):
    d = w_in.shape[0]
    bounds = np.cumsum([0, MLA_Q_LORA, MLA_KV_LORA, MLA_ROPE, MLA_WIDTH, NA_WIDTH, NA_WIDTH, NA_WIDTH, NA_WIDTH])
    cq, ckv, kr, gm, qn, kn, vn, gn = [w_in[:, a:b] for a, b in zip(bounds[:-1], bounds[1:])]
    half = MLA_ROPE // 2
    x1, x2 = kr[:, :half], kr[:, half:]
    kr_group = jnp.concatenate([jnp.zeros((d, MLA_NOPE), F32), x1, x2, x2, x1], axis=1)
    na_scale = NA_HEAD_DIM ** -0.5
    packed = jnp.concatenate([cq, ckv, kr_group, gm, qn * na_scale, kn, vn, gn], axis=1)
    assert packed.shape[1] == _C_END

    uq = w_uq.reshape(MLA_Q_LORA, MLA_HEADS, MLA_NOPE + MLA_ROPE)
    q1, q2 = uq[..., MLA_NOPE:MLA_NOPE + half], uq[..., MLA_NOPE + half:]
    uq = jnp.concatenate([uq[..., :MLA_NOPE], q1, q2, q2, q1], axis=-1).reshape(MLA_Q_LORA, MLA_HEADS * LANES)

    ukv = w_ukv.reshape(MLA_KV_LORA, MLA_HEADS, MLA_NOPE + 64)
    wk = jnp.concatenate([ukv[..., :MLA_NOPE], jnp.zeros_like(ukv[..., :LANES - MLA_NOPE])], axis=-1)
    wk = wk.reshape(MLA_KV_LORA, MLA_HEADS * LANES)
    wv = ukv[..., MLA_NOPE:].reshape(MLA_KV_LORA, MLA_WIDTH)
    return {"w_in": packed.astype(BF16), "q_norm": q_norm.reshape(1, -1), "w_uq": uq.astype(BF16),
            "kv_norm": kv_norm.reshape(1, -1), "w_k": wk.astype(BF16), "w_v": wv.astype(BF16),
            "w_out": w_out.astype(BF16)}


def _rope_tables(s, lc):
    rows = s // GRID_W
    axis_dims = MLA_ROPE // 2
    inv = jnp.power(ROPE_BASE, -jnp.arange(0, axis_dims, 2, dtype=F32) / axis_dims)
    n_freq = inv.shape[0]
    ang_row = jnp.arange(rows).astype(F32)[:, None] * inv
    ang_col = jnp.arange(GRID_W).astype(F32)[:, None] * inv

    def per_token(f):
        by_row = jnp.broadcast_to(f(ang_row)[:, None, :], (rows, GRID_W, n_freq))
        by_col = jnp.broadcast_to(f(ang_col)[None, :, :], (rows, GRID_W, n_freq))
        return jnp.concatenate([by_row, by_col], axis=-1).reshape(s, 2 * n_freq)

    cos, sin = per_token(jnp.cos), per_token(jnp.sin)
    z = lambda n, w: jnp.zeros((n, w), F32)
    pad = LANES - MLA_NOPE - MLA_ROPE
    cos_k = jnp.concatenate([z(s, MLA_NOPE), cos, cos, z(s, pad)], axis=1)
    sin_k = jnp.concatenate([z(s, MLA_NOPE), -sin, sin, z(s, pad)], axis=1)
    ctx_k = jnp.concatenate([z(lc, MLA_NOPE), jnp.ones((lc, MLA_ROPE), F32), z(lc, pad)], axis=1)
    return (cos_k, sin_k), (ctx_k, z(lc, LANES))


def kernel(x, c, ctx, c_ctx, norm_g, w_mod, b_mod, att_w_in, mla_q_norm, mla_w_uq, mla_kv_norm, mla_w_ukv,
           na_rpb, att_w_out, sgu_w_in, sgu_ln_g, sgu_ln_b, sgu_w_s, sgu_b_s, sgu_w_out, final_norm):
    batch, s, d = x.shape
    lc = ctx.shape[1]
    depth = norm_g.shape[0]
    assert batch == 1 and c.shape[0] == 1
    rows = s // GRID_W
    assert s % (NA_GROUPS_PER_STEP * NA_TQ) == 0 and rows >= NA_KROWS

    tm = 512 if s % 512 == 0 else 256
    tq = 1024 if s % 1024 == 0 else 256
    tk = 512

    cond = jnp.concatenate([c, c_ctx[None, :], jnp.zeros((6, d), F32)], axis=0)
    mods = _mod_call(cond, w_mod, b_mod)
    tabs_x, tabs_c = _rope_tables(s, lc)
    fnorm = final_norm.reshape(1, d)

    xs, xc = x[0], ctx[0]
    last_ctx_reader = max(l for l in range(depth) if l % 2 == 0)
    for l in range(depth):
        i = l // 2
        update_ctx = l < last_ctx_reader
        ng = norm_g[l].reshape(1, d)
        if l % 2 == 0:
            w = _pack_att_weights(att_w_in[i], mla_q_norm[i], mla_w_uq[i], mla_kv_norm[i], mla_w_ukv[i],
                                  att_w_out[i])
            bias = _na_bias_slabs(na_rpb[i])
            qx, kx, vx, qnx, knx, vnx, sgx = _att_in_call(xs, mods[l], ng, w, tabs_x, row=0, tm=tm)
            qc, kc, vc, qnc, knc, vnc, sgc = _att_in_call(xc, mods[l], ng, w, tabs_c, row=1, tm=lc)
            mla_x = _mla_call(qx, kx, vx, kc, vc, tq=tq, tk=tk)
            na_x = _na_call(qnx, knx, vnx, knc, vnc, bias, rows=rows)
            xs = _att_out_call(mla_x, na_x, sgx, xs, mods[l], w["w_out"], row=0, tm=tm)
            if update_ctx:
                mla_c = _mla_call(qc, None, None, kc, vc, tq=lc, tk=tk)
                na_c = _pair_attn_call(qnc, knc, vnc)
                xc = _att_out_call(mla_c, na_c, sgc, xc, mods[l], w["w_out"], row=1, tm=lc)
        else:
            w = {"w_in": sgu_w_in[i].astype(BF16), "ln_g": sgu_ln_g[i].reshape(1, -1),
                 "ln_b": sgu_ln_b[i].reshape(1, -1), "w_s": sgu_w_s[i].astype(BF16),
                 "b_s": sgu_b_s[i][:, :, None], "w_out": sgu_w_out[i].astype(BF16)}
            final = l == depth - 1
            xs = _sgu_call(xs, mods[l], ng, w, fnorm, row=0, tm=tm, final=final)
            if update_ctx:
                xc = _sgu_call(xc, mods[l], ng, w, fnorm, row=1, tm=lc, final=False)
    if depth % 2 == 1:
        raise NotImplementedError("final RMSNorm is fused into a trailing spatial-gating layer")
    return xs[None]
```

```python
import functools
import math

import numpy as np
import jax
import jax.numpy as jnp
from jax import lax
from jax.experimental import pallas as pl
from jax.experimental.pallas import tpu as pltpu

F32 = jnp.float32
BF16 = jnp.bfloat16

EPS = 1e-6
NEG = -1e30
LOG2E = math.log2(math.e)
LANES = 128
VMEM_LIMIT = 56 * 1024 * 1024

GRID_W = 64
MLA_HEADS = 8
MLA_NOPE = 64
MLA_ROPE = 32
MLA_V = 64
SUM_ROWS = 16
MLA_UNROLL = 4
MLA_Q_LORA = 768
MLA_KV_LORA = 256
MLA_WIDTH = 512
ROPE_BASE = 10000.0
NA_HEADS = 8
NA_HEAD_DIM = 64
NA_WIDTH = 512
NA_KH = 8
NA_KW = 16
SGU_CHUNK = 128
SGU_WIDTH = 2048
SGU_GROUPS = 8
SGU_GROUP_DIM = SGU_WIDTH // SGU_GROUPS

NA_QROWS = 4
NA_KROWS = 12
NA_TQ = NA_QROWS * GRID_W
NA_GROUPS_PER_STEP = 4
NA_SLABS = 16


def _dot(a, b):
    return jnp.dot(a, b, preferred_element_type=F32)


def _rms(x, g):
    return x * lax.rsqrt(jnp.mean(x * x, axis=-1, keepdims=True) + EPS) * g


def _silu(x):
    return x * jax.nn.sigmoid(x)


def _gelu(x):
    return 0.5 * x * (1.0 + lax.erf(x * np.float32(math.sqrt(0.5))))


def _params(n_axes, vmem=None):
    return pltpu.CompilerParams(dimension_semantics=("arbitrary",) * n_axes,
                                vmem_limit_bytes=vmem)


def _const_spec(shape):
    zeros = (0,) * len(shape)
    return pl.BlockSpec(shape, lambda *_: zeros, pipeline_mode=pl.Buffered(1))


def _mod_kernel(cond_ref, w_ref, b_ref, o_ref):
    s = _silu(cond_ref[...])
    w = w_ref[0]
    s_hi = s.astype(BF16)
    s_lo = (s - s_hi.astype(F32)).astype(BF16)
    w_hi = w.astype(BF16)
    w_lo = (w - w_hi.astype(F32)).astype(BF16)
    o_ref[0] = _dot(s_hi, w_hi) + _dot(s_hi, w_lo) + _dot(s_lo, w_hi) + b_ref[0]


def _mod_call(cond, w_mod, b_mod):
    depth, d, d3 = w_mod.shape
    tn = 1024
    return pl.pallas_call(
        _mod_kernel,
        grid=(depth, d3 // tn),
        in_specs=[pl.BlockSpec((8, d), lambda l, j: (0, 0)),
                  pl.BlockSpec((1, d, tn), lambda l, j: (l, 0, j)),
                  pl.BlockSpec((1, 1, tn), lambda l, j: (l, 0, j))],
        out_specs=pl.BlockSpec((1, 8, tn), lambda l, j: (l, 0, j)),
        out_shape=jax.ShapeDtypeStruct((depth, 8, d3), F32),
        compiler_params=_params(2),
        name="adaln_mod",
    )(cond, w_mod, b_mod.reshape(depth, 1, d3))


_C_CQ = 0
_C_CKV = _C_CQ + MLA_Q_LORA
_C_KR = _C_CKV + MLA_KV_LORA
_C_GM = _C_KR + LANES
_C_QN = _C_GM + MLA_WIDTH
_C_KN = _C_QN + NA_WIDTH
_C_VN = _C_KN + NA_WIDTH
_C_GN = _C_VN + NA_WIDTH
_C_END = _C_GN + NA_WIDTH


def _att_in_kernel(x_ref, mod_ref, ng_ref, win_ref, qnorm_ref, wuq_ref, kvnorm_ref, wk_ref, wv_ref,
                   ck_ref, sk_ref,
                   qt_ref, k_ref, vt_ref, qnt_ref, kn_ref, vnt_ref, sg_ref, *, row):
    d = x_ref.shape[1]
    shift = mod_ref[row:row + 1, 0:d]
    scale = mod_ref[row:row + 1, d:2 * d]
    hb = (_rms(x_ref[...], ng_ref[...]) * (1.0 + scale) + shift).astype(BF16)

    def proj(lo, hi):
        return _dot(hb, win_ref[:, lo:hi])

    cqn = _rms(proj(_C_CQ, _C_CKV), qnorm_ref[...]).astype(BF16)
    q = _dot(cqn, wuq_ref[...])
    q_sw = pltpu.roll(q, q.shape[1] - MLA_ROPE, 1)
    ck_t, sk_t = ck_ref[...], sk_ref[...]
    nope = (lax.broadcasted_iota(jnp.int32, (1, LANES), 1) < MLA_NOPE).astype(F32)
    q_scale = (MLA_NOPE + MLA_ROPE) ** -0.5 * LOG2E
    cq_t, sq_t = (ck_t + nope) * q_scale, sk_t * q_scale
    for h in range(MLA_HEADS):
        sl = slice(h * LANES, (h + 1) * LANES)
        qt_ref[sl, :] = (q[:, sl] * cq_t + q_sw[:, sl] * sq_t).T.astype(BF16)

    ckvn = _rms(proj(_C_CKV, _C_KR), kvnorm_ref[...]).astype(BF16)
    k_nope = _dot(ckvn, wk_ref[...])
    kr = proj(_C_KR, _C_GM)
    kr_rot = kr * ck_t + pltpu.roll(kr, LANES - MLA_ROPE, 1) * sk_t
    for h in range(MLA_HEADS):
        sl = slice(h * LANES, (h + 1) * LANES)
        k_ref[:, sl] = (k_nope[:, sl] + kr_rot).astype(BF16)
    vt_ref[...] = _dot(ckvn, wv_ref[...]).T.astype(BF16)

    qnt_ref[...] = (proj(_C_QN, _C_KN) * LOG2E).T.astype(BF16)
    kn_ref[...] = proj(_C_KN, _C_VN).astype(BF16)
    vnt_ref[...] = proj(_C_VN, _C_GN).T.astype(BF16)
    sg_ref[:, 0:MLA_WIDTH] = _silu(proj(_C_GM, _C_QN)).astype(BF16)
    sg_ref[:, MLA_WIDTH:] = _silu(proj(_C_GN, _C_END)).astype(BF16)


def _att_in_call(x, mod, ng, w, tabs, *, row, tm):
    n, d = x.shape
    hw = MLA_HEADS * LANES
    tok = lambda width: pl.BlockSpec((tm, width), lambda i: (i, 0))
    tok_t = lambda width: pl.BlockSpec((width, tm), lambda i: (0, i))
    out_widths = (hw, hw, MLA_WIDTH, NA_WIDTH, NA_WIDTH, NA_WIDTH, MLA_WIDTH + NA_WIDTH)
    transposed = (True, False, True, True, False, True, False)
    return pl.pallas_call(
        functools.partial(_att_in_kernel, row=row),
        grid=(n // tm,),
        in_specs=[tok(d), _const_spec(mod.shape), _const_spec(ng.shape),
                  _const_spec(w["w_in"].shape), _const_spec(w["q_norm"].shape),
                  _const_spec(w["w_uq"].shape), _const_spec(w["kv_norm"].shape),
                  _const_spec(w["w_k"].shape), _const_spec(w["w_v"].shape),
                  tok(LANES), tok(LANES)],
        out_specs=[tok_t(wd) if t else tok(wd) for wd, t in zip(out_widths, transposed)],
        out_shape=[jax.ShapeDtypeStruct((wd, n) if t else (n, wd), BF16)
                   for wd, t in zip(out_widths, transposed)],
        compiler_params=_params(1, VMEM_LIMIT),
        name="att_in_ctx" if row else "att_in",
    )(x, mod, ng, w["w_in"], w["q_norm"], w["w_uq"], w["kv_norm"], w["w_k"], w["w_v"], *tabs)


def _mla_kernel(*refs, tk, n_chunks):
    heads = (0, 1)
    if n_chunks:
        qt_ref, kx_ref, vxt_ref, kc_ref, vct_ref, o_ref = refs[:6]
        acc_refs, s_refs = refs[6:8], refs[8:12]
    else:
        qt_ref, kc_ref, vct_ref, o_ref = refs[:4]
        acc_refs = refs[4:6]
    tq = qt_ref.shape[1]
    hs = [slice(e * LANES, (e + 1) * LANES) for e in heads]
    vs = [slice(e * MLA_V, (e + 1) * MLA_V) for e in heads]

    def chunk(j):
        return pl.ds(j * tk if isinstance(j, int) else pl.multiple_of(j * tk, tk), tk)

    def scores(e, j):
        return _dot(kx_ref[chunk(j), hs[e]], qt_ref[hs[e], :])

    def ctx_scores(e):
        return _dot(kc_ref[:, hs[e]], qt_ref[hs[e], :])

    def values(e, j):
        return vxt_ref[vs[e], chunk(j)]

    def update(e, s, vt, m):
        m_new = jnp.maximum(m, jnp.max(s, axis=0, keepdims=True))
        alpha = jnp.exp2(m - m_new)
        p = jnp.exp2(s - m_new).astype(BF16)
        ones_row = (lax.broadcasted_iota(jnp.int32, (SUM_ROWS, vt.shape[1]), 0) == 0).astype(BF16)
        acc = acc_refs[e]
        acc[...] = alpha * acc[...] + _dot(jnp.concatenate([vt, ones_row], axis=0), p)
        return m_new

    for e in heads:
        acc_refs[e][...] = jnp.zeros(acc_refs[e].shape, F32)
    ms = [jnp.full((1, tq), NEG, F32) for _ in heads]
    if n_chunks:
        s_ctx = [None, None]

        def step(c, parity, ms, last=False):
            ms = list(ms)
            for e in heads:
                if last:
                    s_ctx[e] = ctx_scores(e)
                else:
                    s_refs[2 * e + 1 - parity][...] = scores(e, c + 1)
                ms[e] = update(e, s_refs[2 * e + parity][...], values(e, c), ms[e])
            return ms

        for e in heads:
            s_refs[2 * e][...] = scores(e, 0)
        trips = (n_chunks - 1) // MLA_UNROLL

        def body(t, ms):
            for i in range(MLA_UNROLL):
                ms = step(MLA_UNROLL * t + i, i % 2, ms)
            return tuple(ms)
        ms = list(lax.fori_loop(0, trips, body, tuple(ms)))
        for c in range(trips * MLA_UNROLL, n_chunks):
            ms = step(c, c % 2, ms, last=c == n_chunks - 1)
    else:
        s_ctx = [ctx_scores(e) for e in heads]
    for e in heads:
        update(e, s_ctx[e], vct_ref[vs[e], :], ms[e])
    outs = [acc_refs[e][0:MLA_V, :] / acc_refs[e][MLA_V:MLA_V + 1, :] for e in heads]
    o_ref[...] = jnp.concatenate(outs, axis=0).T.astype(BF16)


def _mla_call(qt, kx, vxt, kc, vct, *, tq, tk):
    n = qt.shape[1]
    lc = kc.shape[0]
    pairs = MLA_HEADS // 2
    in_specs = [pl.BlockSpec((2 * LANES, tq), lambda hp, i: (hp, i))]
    args = [qt]
    n_chunks = 0
    if kx is not None:
        s = kx.shape[0]
        n_chunks = s // tk
        assert s % tk == 0 and n_chunks % 2 == 0
        in_specs += [pl.BlockSpec((s, 2 * LANES), lambda hp, i: (0, hp)),
                     pl.BlockSpec((LANES, s), lambda hp, i: (hp, 0))]
        args += [kx, vxt]
    in_specs += [pl.BlockSpec((lc, 2 * LANES), lambda hp, i: (0, hp)),
                 pl.BlockSpec((LANES, lc), lambda hp, i: (hp, 0))]
    args += [kc, vct]
    return pl.pallas_call(
        functools.partial(_mla_kernel, tk=tk, n_chunks=n_chunks),
        grid=(pairs, n // tq),
        in_specs=in_specs,
        out_specs=pl.BlockSpec((tq, LANES), lambda hp, i: (i, hp)),
        out_shape=jax.ShapeDtypeStruct((n, MLA_WIDTH), BF16),
        scratch_shapes=[pltpu.VMEM((MLA_V + SUM_ROWS, tq), F32)] * 2
        + [pltpu.VMEM((tk, tq), F32)] * (4 if n_chunks else 0),
        compiler_params=_params(2, VMEM_LIMIT),
        name="mla_flash" if kx is not None else "mla_ctx",
    )(*args)


def _head_rows(qt, e):
    row = lax.broadcasted_iota(jnp.int32, qt.shape, 0)
    keep = (row >= e * NA_HEAD_DIM) & (row < (e + 1) * NA_HEAD_DIM)
    return jnp.where(keep, qt, jnp.zeros_like(qt))


def _with_ones_row(vt):
    ones_row = (lax.broadcasted_iota(jnp.int32, (SUM_ROWS, vt.shape[1]), 0) == 0).astype(BF16)
    return jnp.concatenate([vt, ones_row], axis=0)


def _softmax_pv(s_list, vt_list):
    m = s_list[0].max(axis=0, keepdims=True)
    for s in s_list[1:]:
        m = jnp.maximum(m, s.max(axis=0, keepdims=True))
    acc = None
    for s, vt in zip(s_list, vt_list):
        part = _dot(_with_ones_row(vt), jnp.exp2(s - m).astype(BF16))
        acc = part if acc is None else acc + part
    return acc[0:NA_HEAD_DIM, :] / acc[NA_HEAD_DIM:NA_HEAD_DIM + 1, :]


def _na_kernel(qt_ref, kn_ref, vnt_ref, kc_ref, vct_ref, bias_ref, o_ref, *, rows):
    n_sub = qt_ref.shape[1] // NA_TQ
    blk = pl.program_id(1)
    lane_row = lax.broadcasted_iota(jnp.int32, (1, NA_TQ), 1) // GRID_W
    n_win = NA_KROWS * GRID_W

    def geometry(u):
        r = (blk * n_sub + u) * NA_QROWS
        kr0 = jnp.clip(r - NA_KH // 2, 0, rows - NA_KROWS)
        return r, kr0, pl.multiple_of(kr0 * GRID_W, 2 * GRID_W)

    def raw_scores(u, e):
        qt = _head_rows(qt_ref[:, u * NA_TQ:(u + 1) * NA_TQ], e)
        start = geometry(u)[2]
        return _dot(kn_ref[pl.ds(start, n_win), :], qt), _dot(kc_ref[...], qt)

    def finish(u, e, s, s_ctx):
        r, kr0, start = geometry(u)
        win_lo = jnp.clip(r + lane_row - NA_KH // 2, 0, rows - NA_KH)
        tiles = []
        for j in range(NA_KROWS):
            kr = kr0 + j
            in_win = (kr >= win_lo) & (kr < win_lo + NA_KH)
            slabs = [bias_ref[e, jnp.clip(kr - (r + 2 * cp) + NA_KH - 1, 0, NA_SLABS - 1)]
                     for cp in range(NA_QROWS // 2)]
            bias = jnp.concatenate(slabs, axis=1) + jnp.where(in_win, 0.0, NEG)
            tiles.append(s[j * GRID_W:(j + 1) * GRID_W, :] + bias)
        vs = slice(e * NA_HEAD_DIM, (e + 1) * NA_HEAD_DIM)
        return _softmax_pv([jnp.concatenate(tiles, axis=0), s_ctx],
                           [vnt_ref[vs, pl.ds(start, n_win)], vct_ref[vs, :]])

    order = [(u, e) for u in range(n_sub) for e in range(2)]
    outs = {}
    pending = raw_scores(*order[0])
    for i, (u, e) in enumerate(order):
        current = pending
        if i + 1 < len(order):
            pending = raw_scores(*order[i + 1])
        outs[u, e] = finish(u, e, *current)
    o_t = jnp.concatenate([jnp.concatenate([outs[u, e] for u in range(n_sub)], axis=1) for e in range(2)], axis=0)
    o_ref[...] = o_t.T.astype(BF16)


def _na_call(qnt, kn, vnt, knc, vnct, bias, *, rows):
    n = qnt.shape[1]
    lc = knc.shape[0]
    tq = NA_GROUPS_PER_STEP * NA_TQ
    pairs = NA_HEADS // 2
    return pl.pallas_call(
        functools.partial(_na_kernel, rows=rows),
        grid=(pairs, n // tq),
        in_specs=[pl.BlockSpec((LANES, tq), lambda hp, b: (hp, b)),
                  pl.BlockSpec((n, LANES), lambda hp, b: (0, hp)),
                  pl.BlockSpec((LANES, n), lambda hp, b: (hp, 0)),
                  pl.BlockSpec((lc, LANES), lambda hp, b: (0, hp)),
                  pl.BlockSpec((LANES, lc), lambda hp, b: (hp, 0)),
                  pl.BlockSpec((2, NA_SLABS, GRID_W, 2 * GRID_W), lambda hp, b: (hp, 0, 0, 0))],
        out_specs=pl.BlockSpec((tq, LANES), lambda hp, b: (b, hp)),
        out_shape=jax.ShapeDtypeStruct((n, NA_WIDTH), BF16),
        compiler_params=_params(2, VMEM_LIMIT),
        name="na_window",
    )(qnt, kn, vnt, knc, vnct, bias)


def _pair_attn_kernel(qt_ref, k_ref, vt_ref, o_ref):
    outs = []
    for e in range(2):
        vs = slice(e * NA_HEAD_DIM, (e + 1) * NA_HEAD_DIM)
        outs.append(_softmax_pv([_dot(k_ref[...], _head_rows(qt_ref[...], e))], [vt_ref[vs, :]]))
    o_ref[...] = jnp.concatenate(outs, axis=0).T.astype(BF16)


def _pair_attn_call(qt, k, vt):
    n = qt.shape[1]
    fm = lambda cols: pl.BlockSpec((LANES, cols), lambda hp: (hp, 0))
    return pl.pallas_call(
        _pair_attn_kernel,
        grid=(NA_HEADS // 2,),
        in_specs=[fm(n), pl.BlockSpec((k.shape[0], LANES), lambda hp: (0, hp)), fm(vt.shape[1])],
        out_specs=pl.BlockSpec((n, LANES), lambda hp: (0, hp)),
        out_shape=jax.ShapeDtypeStruct((n, NA_WIDTH), BF16),
        compiler_params=_params(1),
        name="na_ctx",
    )(qt, k, vt)


def _na_bias_slabs(rpb):
    w, kw = GRID_W, NA_KW
    c = np.arange(w)[None, :]
    kc = np.arange(w)[:, None]
    c0 = np.clip(c - kw // 2, 0, w - kw)
    col_valid = (kc >= c0) & (kc < c0 + kw)
    col_idx = np.clip(kc - c + (kw - 1), 0, 2 * kw - 2)
    h, n_off, n_col = rpb.shape
    onehot = (col_idx.reshape(1, -1) == np.arange(n_col)[:, None]).astype(np.float32)
    by_off = jnp.dot(rpb.reshape(h * n_off, n_col) * LOG2E, onehot, precision=lax.Precision.HIGHEST)
    by_off = jnp.where(col_valid[None, None], by_off.reshape(h, n_off, w, w), NEG)
    fill = jnp.full((h, NA_SLABS - n_off, w, w), NEG, F32)
    by_off = jnp.concatenate([by_off, fill], axis=1)
    prev = jnp.concatenate([jnp.full((h, 1, w, w), NEG, F32), by_off[:, :-1]], axis=1)
    return jnp.concatenate([by_off, prev], axis=-1)


def _att_sgu_kernel(x_ref, mla_ref, na_ref, sg_ref, amod_ref, awout_ref,
                    mod_ref, ng_ref, win_ref, lng_ref, lnb_ref, ws_ref, bs_ref, wout_ref, fn_ref,
                    o_ref, *, row, final):
    tm, d = x_ref.shape
    a = (mla_ref[...].astype(F32) * sg_ref[:, 0:MLA_WIDTH].astype(F32)).astype(BF16)
    b = (na_ref[...].astype(F32) * sg_ref[:, MLA_WIDTH:].astype(F32)).astype(BF16)
    y_att = _dot(a, awout_ref[0:MLA_WIDTH, :]) + _dot(b, awout_ref[MLA_WIDTH:, :])
    x = x_ref[...] + amod_ref[row:row + 1, 2 * d:3 * d] * y_att

    shift = mod_ref[row:row + 1, 0:d]
    scale = mod_ref[row:row + 1, d:2 * d]
    gate = mod_ref[row:row + 1, 2 * d:3 * d]
    hb = (_rms(x, ng_ref[...]) * (1.0 + scale) + shift).astype(BF16)

    v = _gelu(_dot(hb, win_ref[:, SGU_WIDTH:2 * SGU_WIDTH]))
    mu = jnp.mean(v, axis=-1, keepdims=True)
    vc = v - mu
    var = jnp.mean(vc * vc, axis=-1, keepdims=True)
    vn = (vc * lax.rsqrt(var + EPS) * lng_ref[...] + lnb_ref[...]).astype(BF16)

    y = jnp.zeros((tm, d), F32)
    for g in range(SGU_GROUPS):
        cols = slice(g * SGU_GROUP_DIM, (g + 1) * SGU_GROUP_DIM)
        u = _gelu(_dot(hb, win_ref[:, g * SGU_GROUP_DIM:(g + 1) * SGU_GROUP_DIM]))
        gt = _silu(_dot(hb, win_ref[:, 2 * SGU_WIDTH + g * SGU_GROUP_DIM:
                                    2 * SGU_WIDTH + (g + 1) * SGU_GROUP_DIM]))
        sv = jnp.concatenate(
            [_dot(ws_ref[g], vn[c * SGU_CHUNK:(c + 1) * SGU_CHUNK, cols]) + bs_ref[g]
             for c in range(tm // SGU_CHUNK)], axis=0)
        y = y + _dot((u * sv * gt).astype(BF16), wout_ref[cols, :])
    out = x + gate * y
    if final:
        out = _rms(out, fn_ref[...])
    o_ref[...] = out


def _att_sgu_call(x, att, mod, ng, w, final_norm, *, row, tm, final):
    n, d = x.shape
    mla, na, sg, att_mod, att_w_out = att
    tok = lambda width: pl.BlockSpec((tm, width), lambda i: (i, 0))
    consts = [mod, ng, w["w_in"], w["ln_g"], w["ln_b"], w["w_s"], w["b_s"], w["w_out"], final_norm]
    return pl.pallas_call(
        functools.partial(_att_sgu_kernel, row=row, final=final),
        grid=(n // tm,),
        in_specs=[tok(d), tok(MLA_WIDTH), tok(NA_WIDTH), tok(MLA_WIDTH + NA_WIDTH),
                  _const_spec(att_mod.shape), _const_spec(att_w_out.shape)]
        + [_const_spec(a.shape) for a in consts],
        out_specs=tok(d),
        out_shape=jax.ShapeDtypeStruct((n, d), F32),
        compiler_params=_params(1, VMEM_LIMIT),
        name="att_sgu_ctx" if row else "att_sgu",
    )(x, mla, na, sg, att_mod, att_w_out, *consts)


def _pack_att_weights(w_in, q_norm, w_uq, kv_norm, w_ukv, w_out):
    d = w_in.shape[0]
    bounds = np.cumsum([0, MLA_Q_LORA, MLA_KV_LORA, MLA_ROPE, MLA_WIDTH, NA_WIDTH, NA_WIDTH, NA_WIDTH, NA_WIDTH])
    cq, ckv, kr, gm, qn, kn, vn, gn = [w_in[:, a:b] for a, b in zip(bounds[:-1], bounds[1:])]
    half = MLA_ROPE // 2
    x1, x2 = kr[:, :half], kr[:, half:]
    kr_group = jnp.concatenate([jnp.zeros((d, MLA_NOPE), F32), x1, x2, x2, x1], axis=1)
    na_scale = NA_HEAD_DIM ** -0.5
    packed = jnp.concatenate([cq, ckv, kr_group, gm, qn * na_scale, kn, vn, gn], axis=1)
    assert packed.shape[1] == _C_END

    uq = w_uq.reshape(MLA_Q_LORA, MLA_HEADS, MLA_NOPE + MLA_ROPE)
    q1, q2 = uq[..., MLA_NOPE:MLA_NOPE + half], uq[..., MLA_NOPE + half:]
    uq = jnp.concatenate([uq[..., :MLA_NOPE], q1, q2, q2, q1], axis=-1).reshape(MLA_Q_LORA, MLA_HEADS * LANES)

    ukv = w_ukv.reshape(MLA_KV_LORA, MLA_HEADS, MLA_NOPE + 64)
    wk = jnp.concatenate([ukv[..., :MLA_NOPE], jnp.zeros_like(ukv[..., :LANES - MLA_NOPE])], axis=-1)
    wk = wk.reshape(MLA_KV_LORA, MLA_HEADS * LANES)
    wv = ukv[..., MLA_NOPE:].reshape(MLA_KV_LORA, MLA_WIDTH)
    return {"w_in": packed.astype(BF16), "q_norm": q_norm.reshape(1, -1), "w_uq": uq.astype(BF16),
            "kv_norm": kv_norm.reshape(1, -1), "w_k": wk.astype(BF16), "w_v": wv.astype(BF16),
            "w_out": w_out.astype(BF16)}


def _rope_tables(s, lc):
    rows = s // GRID_W
    axis_dims = MLA_ROPE // 2
    inv = jnp.power(ROPE_BASE, -jnp.arange(0, axis_dims, 2, dtype=F32) / axis_dims)
    n_freq = inv.shape[0]
    ang_row = jnp.arange(rows).astype(F32)[:, None] * inv
    ang_col = jnp.arange(GRID_W).astype(F32)[:, None] * inv

    def per_token(f):
        by_row = jnp.broadcast_to(f(ang_row)[:, None, :], (rows, GRID_W, n_freq))
        by_col = jnp.broadcast_to(f(ang_col)[None, :, :], (rows, GRID_W, n_freq))
        return jnp.concatenate([by_row, by_col], axis=-1).reshape(s, 2 * n_freq)

    cos, sin = per_token(jnp.cos), per_token(jnp.sin)
    z = lambda n, w: jnp.zeros((n, w), F32)
    pad = LANES - MLA_NOPE - MLA_ROPE
    cos_k = jnp.concatenate([z(s, MLA_NOPE), cos, cos, z(s, pad)], axis=1)
    sin_k = jnp.concatenate([z(s, MLA_NOPE), -sin, sin, z(s, pad)], axis=1)
    ctx_k = jnp.concatenate([z(lc, MLA_NOPE), jnp.ones((lc, MLA_ROPE), F32), z(lc, pad)], axis=1)
    return (cos_k, sin_k), (ctx_k, z(lc, LANES))


def kernel(x, c, ctx, c_ctx, norm_g, w_mod, b_mod, att_w_in, mla_q_norm, mla_w_uq, mla_kv_norm, mla_w_ukv,
           na_rpb, att_w_out, sgu_w_in, sgu_ln_g, sgu_ln_b, sgu_w_s, sgu_b_s, sgu_w_out, final_norm):
    batch, s, d = x.shape
    lc = ctx.shape[1]
    depth = norm_g.shape[0]
    assert batch == 1 and c.shape[0] == 1
    rows = s // GRID_W
    assert s % (NA_GROUPS_PER_STEP * NA_TQ) == 0 and rows >= NA_KROWS

    tm = 512 if s % 512 == 0 else 256
    tq = 1024 if s % 1024 == 0 else 256
    tk = 512

    cond = jnp.concatenate([c, c_ctx[None, :], jnp.zeros((6, d), F32)], axis=0)
    mods = _mod_call(cond, w_mod, b_mod)
    tabs_x, tabs_c = _rope_tables(s, lc)
    fnorm = final_norm.reshape(1, d)

    xs, xc = x[0], ctx[0]
    last_ctx_reader = max(l for l in range(depth) if l % 2 == 0)
    att_x = att_c = None
    for l in range(depth):
        i = l // 2
        update_ctx = l < last_ctx_reader
        ng = norm_g[l].reshape(1, d)
        if l % 2 == 0:
            if l == depth - 1:
                raise NotImplementedError("the block must end with a spatial gating layer")
            w = _pack_att_weights(att_w_in[i], mla_q_norm[i], mla_w_uq[i], mla_kv_norm[i], mla_w_ukv[i],
                                  att_w_out[i])
            bias = _na_bias_slabs(na_rpb[i])
            qx, kx, vx, qnx, knx, vnx, sgx = _att_in_call(xs, mods[l], ng, w, tabs_x, row=0, tm=tm)
            qc, kc, vc, qnc, knc, vnc, sgc = _att_in_call(xc, mods[l], ng, w, tabs_c, row=1, tm=lc)
            mla_x = _mla_call(qx, kx, vx, kc, vc, tq=tq, tk=tk)
            na_x = _na_call(qnx, knx, vnx, knc, vnc, bias, rows=rows)
            att_x = (mla_x, na_x, sgx, mods[l], w["w_out"])
            if update_ctx:
                mla_c = _mla_call(qc, None, None, kc, vc, tq=lc, tk=tk)
                na_c = _pair_attn_call(qnc, knc, vnc)
                att_c = (mla_c, na_c, sgc, mods[l], w["w_out"])
        else:
            w = {"w_in": sgu_w_in[i].astype(BF16), "ln_g": sgu_ln_g[i].reshape(1, -1),
                 "ln_b": sgu_ln_b[i].reshape(1, -1), "w_s": sgu_w_s[i].astype(BF16),
                 "b_s": sgu_b_s[i][:, :, None], "w_out": sgu_w_out[i].astype(BF16)}
            final = l == depth - 1
            xs = _att_sgu_call(xs, att_x, mods[l], ng, w, fnorm, row=0, tm=tm, final=final)
            if update_ctx:
                xc = _att_sgu_call(xc, att_c, mods[l], ng, w, fnorm, row=1, tm=lc, final=False)
    return xs[None]
```

```python
import functools
import math

import numpy as np
import jax
import jax.numpy as jnp
from jax import lax
from jax.experimental import pallas as pl
from jax.experimental.pallas import tpu as pltpu

F32 = jnp.float32
BF16 = jnp.bfloat16

EPS = 1e-6
NEG = -1e30
LOG2E = math.log2(math.e)
LANES = 128
VMEM_LIMIT = 56 * 1024 * 1024

GRID_W = 64
MLA_HEADS = 8
MLA_NOPE = 64
MLA_ROPE = 32
MLA_V = 64
SUM_ROWS = 16
MLA_UNROLL = 4
MLA_Q_LORA = 768
MLA_KV_LORA = 256
MLA_WIDTH = 512
ROPE_BASE = 10000.0
NA_HEADS = 8
NA_HEAD_DIM = 64
NA_WIDTH = 512
NA_KH = 8
NA_KW = 16
SGU_CHUNK = 128
SGU_WIDTH = 2048
SGU_GROUPS = 8
SGU_GROUP_DIM = SGU_WIDTH // SGU_GROUPS

NA_QROWS = 4
NA_KROWS = 12
NA_TQ = NA_QROWS * GRID_W
NA_GROUPS_PER_STEP = 4
NA_SLABS = 16


def _dot(a, b):
    return jnp.dot(a, b, preferred_element_type=F32)


def _rms(x, g):
    return x * lax.rsqrt(jnp.mean(x * x, axis=-1, keepdims=True) + EPS) * g


def _silu(x):
    return x * jax.nn.sigmoid(x)


def _gelu(x):
    return 0.5 * x * (1.0 + lax.erf(x * np.float32(math.sqrt(0.5))))


def _params(n_axes, vmem=None):
    return pltpu.CompilerParams(dimension_semantics=("arbitrary",) * n_axes,
                                vmem_limit_bytes=vmem)


def _const_spec(shape):
    zeros = (0,) * len(shape)
    return pl.BlockSpec(shape, lambda *_: zeros, pipeline_mode=pl.Buffered(1))


def _mod_kernel(cond_ref, w_ref, b_ref, o_ref):
    s = _silu(cond_ref[...])
    w = w_ref[0]
    s_hi = s.astype(BF16)
    s_lo = (s - s_hi.astype(F32)).astype(BF16)
    w_hi = w.astype(BF16)
    w_lo = (w - w_hi.astype(F32)).astype(BF16)
    o_ref[0] = _dot(s_hi, w_hi) + _dot(s_hi, w_lo) + _dot(s_lo, w_hi) + b_ref[0]


def _mod_call(cond, w_mod, b_mod):
    depth, d, d3 = w_mod.shape
    tn = 1024
    return pl.pallas_call(
        _mod_kernel,
        grid=(depth, d3 // tn),
        in_specs=[pl.BlockSpec((8, d), lambda l, j: (0, 0)),
                  pl.BlockSpec((1, d, tn), lambda l, j: (l, 0, j)),
                  pl.BlockSpec((1, 1, tn), lambda l, j: (l, 0, j))],
        out_specs=pl.BlockSpec((1, 8, tn), lambda l, j: (l, 0, j)),
        out_shape=jax.ShapeDtypeStruct((depth, 8, d3), F32),
        compiler_params=_params(2),
        name="adaln_mod",
    )(cond, w_mod, b_mod.reshape(depth, 1, d3))


_C_CQ = 0
_C_CKV = _C_CQ + MLA_Q_LORA
_C_KR = _C_CKV + MLA_KV_LORA
_C_GM = _C_KR + LANES
_C_QN = _C_GM + MLA_WIDTH
_C_KN = _C_QN + NA_WIDTH
_C_VN = _C_KN + NA_WIDTH
_C_GN = _C_VN + NA_WIDTH
_C_END = _C_GN + NA_WIDTH


def _att_in_kernel(x_ref, mod_ref, ng_ref, win_ref, qnorm_ref, wuq_ref, kvnorm_ref, wk_ref, wv_ref,
                   ck_ref, sk_ref,
                   qt_ref, k_ref, vt_ref, qnt_ref, kn_ref, vnt_ref, sg_ref, *, row):
    d = x_ref.shape[1]
    shift = mod_ref[row:row + 1, 0:d]
    scale = mod_ref[row:row + 1, d:2 * d]
    hb = (_rms(x_ref[...], ng_ref[...]) * (1.0 + scale) + shift).astype(BF16)

    def proj(lo, hi):
        return _dot(hb, win_ref[:, lo:hi])

    cqn = _rms(proj(_C_CQ, _C_CKV), qnorm_ref[...]).astype(BF16)
    q = _dot(cqn, wuq_ref[...])
    q_sw = pltpu.roll(q, q.shape[1] - MLA_ROPE, 1)
    ck_t, sk_t = ck_ref[...], sk_ref[...]
    nope = (lax.broadcasted_iota(jnp.int32, (1, LANES), 1) < MLA_NOPE).astype(F32)
    q_scale = (MLA_NOPE + MLA_ROPE) ** -0.5 * LOG2E
    cq_t, sq_t = (ck_t + nope) * q_scale, sk_t * q_scale
    for h in range(MLA_HEADS):
        sl = slice(h * LANES, (h + 1) * LANES)
        qt_ref[sl, :] = (q[:, sl] * cq_t + q_sw[:, sl] * sq_t).T.astype(BF16)

    ckvn = _rms(proj(_C_CKV, _C_KR), kvnorm_ref[...]).astype(BF16)
    k_nope = _dot(ckvn, wk_ref[...])
    kr = proj(_C_KR, _C_GM)
    kr_rot = kr * ck_t + pltpu.roll(kr, LANES - MLA_ROPE, 1) * sk_t
    for h in range(MLA_HEADS):
        sl = slice(h * LANES, (h + 1) * LANES)
        k_ref[:, sl] = (k_nope[:, sl] + kr_rot).astype(BF16)
    vt_ref[...] = _dot(ckvn, wv_ref[...]).T.astype(BF16)

    qnt_ref[...] = (proj(_C_QN, _C_KN) * LOG2E).T.astype(BF16)
    kn_ref[...] = proj(_C_KN, _C_VN).astype(BF16)
    vnt_ref[...] = proj(_C_VN, _C_GN).T.astype(BF16)
    sg_ref[:, 0:MLA_WIDTH] = _silu(proj(_C_GM, _C_QN)).astype(BF16)
    sg_ref[:, MLA_WIDTH:] = _silu(proj(_C_GN, _C_END)).astype(BF16)


def _att_in_call(x, mod, ng, w, tabs, *, row, tm):
    n, d = x.shape
    hw = MLA_HEADS * LANES
    tok = lambda width: pl.BlockSpec((tm, width), lambda i: (i, 0))
    tok_t = lambda width: pl.BlockSpec((width, tm), lambda i: (0, i))
    out_widths = (hw, hw, MLA_WIDTH, NA_WIDTH, NA_WIDTH, NA_WIDTH, MLA_WIDTH + NA_WIDTH)
    transposed = (True, False, True, True, False, True, False)
    return pl.pallas_call(
        functools.partial(_att_in_kernel, row=row),
        grid=(n // tm,),
        in_specs=[tok(d), _const_spec(mod.shape), _const_spec(ng.shape),
                  _const_spec(w["w_in"].shape), _const_spec(w["q_norm"].shape),
                  _const_spec(w["w_uq"].shape), _const_spec(w["kv_norm"].shape),
                  _const_spec(w["w_k"].shape), _const_spec(w["w_v"].shape),
                  tok(LANES), tok(LANES)],
        out_specs=[tok_t(wd) if t else tok(wd) for wd, t in zip(out_widths, transposed)],
        out_shape=[jax.ShapeDtypeStruct((wd, n) if t else (n, wd), BF16)
                   for wd, t in zip(out_widths, transposed)],
        compiler_params=_params(1, VMEM_LIMIT),
        name="att_in_ctx" if row else "att_in",
    )(x, mod, ng, w["w_in"], w["q_norm"], w["w_uq"], w["kv_norm"], w["w_k"], w["w_v"], *tabs)


def _mla_kernel(*refs, tk, n_chunks):
    heads = (0, 1)
    if n_chunks:
        qt_ref, kx_ref, vxt_ref, kc_ref, vct_ref, o_ref = refs[:6]
        acc_refs, s_refs = refs[6:8], refs[8:12]
    else:
        qt_ref, kc_ref, vct_ref, o_ref = refs[:4]
        acc_refs = refs[4:6]
    tq = qt_ref.shape[1]
    hs = [slice(e * LANES, (e + 1) * LANES) for e in heads]
    vs = [slice(e * MLA_V, (e + 1) * MLA_V) for e in heads]

    def chunk(j):
        return pl.ds(j * tk if isinstance(j, int) else pl.multiple_of(j * tk, tk), tk)

    def scores(e, j):
        return _dot(kx_ref[chunk(j), hs[e]], qt_ref[hs[e], :])

    def ctx_scores(e):
        return _dot(kc_ref[:, hs[e]], qt_ref[hs[e], :])

    def values(e, j):
        return vxt_ref[vs[e], chunk(j)]

    def col_max(s):
        return jnp.max(s, axis=0, keepdims=True)

    def update(e, s, s_max, vt, m):
        m_new = jnp.maximum(m, s_max)
        alpha = jnp.exp2(m - m_new)
        p = jnp.exp2(s - m_new).astype(BF16)
        ones_row = (lax.broadcasted_iota(jnp.int32, (SUM_ROWS, vt.shape[1]), 0) == 0).astype(BF16)
        acc = acc_refs[e]
        acc[...] = alpha * acc[...] + _dot(jnp.concatenate([vt, ones_row], axis=0), p)
        return m_new

    for e in heads:
        acc_refs[e][...] = jnp.zeros(acc_refs[e].shape, F32)
    ms = [jnp.full((1, tq), NEG, F32) for _ in heads]
    if n_chunks:
        s_ctx = [None, None]

        def issue(e, buf, c):
            s = scores(e, c)
            s_refs[2 * e + buf][...] = s
            return col_max(s)

        def step(c, parity, ms, maxes, last=False):
            ms, maxes = list(ms), list(maxes)
            for e in heads:
                cur_max = maxes[e]
                if last:
                    s_ctx[e] = ctx_scores(e)
                else:
                    maxes[e] = issue(e, 1 - parity, c + 1)
                ms[e] = update(e, s_refs[2 * e + parity][...], cur_max, values(e, c), ms[e])
            return ms, maxes

        maxes = [issue(e, 0, 0) for e in heads]
        trips = (n_chunks - 1) // MLA_UNROLL

        def body(t, carry):
            ms, maxes = carry[:2], carry[2:]
            for i in range(MLA_UNROLL):
                ms, maxes = step(MLA_UNROLL * t + i, i % 2, ms, maxes)
            return tuple(ms) + tuple(maxes)
        carry = lax.fori_loop(0, trips, body, tuple(ms) + tuple(maxes))
        ms, maxes = list(carry[:2]), list(carry[2:])
        for c in range(trips * MLA_UNROLL, n_chunks):
            ms, maxes = step(c, c % 2, ms, maxes, last=c == n_chunks - 1)
    else:
        s_ctx = [ctx_scores(e) for e in heads]
    for e in heads:
        update(e, s_ctx[e], col_max(s_ctx[e]), vct_ref[vs[e], :], ms[e])
    outs = [acc_refs[e][0:MLA_V, :] / acc_refs[e][MLA_V:MLA_V + 1, :] for e in heads]
    o_ref[...] = jnp.concatenate(outs, axis=0).T.astype(BF16)


def _mla_call(qt, kx, vxt, kc, vct, *, tq, tk):
    n = qt.shape[1]
    lc = kc.shape[0]
    pairs = MLA_HEADS // 2
    in_specs = [pl.BlockSpec((2 * LANES, tq), lambda hp, i: (hp, i))]
    args = [qt]
    n_chunks = 0
    if kx is not None:
        s = kx.shape[0]
        n_chunks = s // tk
        assert s % tk == 0 and n_chunks % 2 == 0
        in_specs += [pl.BlockSpec((s, 2 * LANES), lambda hp, i: (0, hp)),
                     pl.BlockSpec((LANES, s), lambda hp, i: (hp, 0))]
        args += [kx, vxt]
    in_specs += [pl.BlockSpec((lc, 2 * LANES), lambda hp, i: (0, hp)),
                 pl.BlockSpec((LANES, lc), lambda hp, i: (hp, 0))]
    args += [kc, vct]
    return pl.pallas_call(
        functools.partial(_mla_kernel, tk=tk, n_chunks=n_chunks),
        grid=(pairs, n // tq),
        in_specs=in_specs,
        out_specs=pl.BlockSpec((tq, LANES), lambda hp, i: (i, hp)),
        out_shape=jax.ShapeDtypeStruct((n, MLA_WIDTH), BF16),
        scratch_shapes=[pltpu.VMEM((MLA_V + SUM_ROWS, tq), F32)] * 2
        + [pltpu.VMEM((tk, tq), F32)] * (4 if n_chunks else 0),
        compiler_params=_params(2, VMEM_LIMIT),
        name="mla_flash" if kx is not None else "mla_ctx",
    )(*args)


def _head_rows(qt, e):
    row = lax.broadcasted_iota(jnp.int32, qt.shape, 0)
    keep = (row >= e * NA_HEAD_DIM) & (row < (e + 1) * NA_HEAD_DIM)
    return jnp.where(keep, qt, jnp.zeros_like(qt))


def _with_ones_row(vt):
    ones_row = (lax.broadcasted_iota(jnp.int32, (SUM_ROWS, vt.shape[1]), 0) == 0).astype(BF16)
    return jnp.concatenate([vt, ones_row], axis=0)


def _softmax_pv(s_list, vt_list):
    m = s_list[0].max(axis=0, keepdims=True)
    for s in s_list[1:]:
        m = jnp.maximum(m, s.max(axis=0, keepdims=True))
    acc = None
    for s, vt in zip(s_list, vt_list):
        part = _dot(_with_ones_row(vt), jnp.exp2(s - m).astype(BF16))
        acc = part if acc is None else acc + part
    return acc[0:NA_HEAD_DIM, :] / acc[NA_HEAD_DIM:NA_HEAD_DIM + 1, :]


def _na_kernel(qt_ref, kn_ref, vnt_ref, kc_ref, vct_ref, bias_ref, o_ref, *, rows):
    n_sub = qt_ref.shape[1] // NA_TQ
    blk = pl.program_id(1)
    lane_row = lax.broadcasted_iota(jnp.int32, (1, NA_TQ), 1) // GRID_W
    n_win = NA_KROWS * GRID_W

    def geometry(u):
        r = (blk * n_sub + u) * NA_QROWS
        kr0 = jnp.clip(r - NA_KH // 2, 0, rows - NA_KROWS)
        return r, kr0, pl.multiple_of(kr0 * GRID_W, 2 * GRID_W)

    def raw_scores(u, e):
        qt = _head_rows(qt_ref[:, u * NA_TQ:(u + 1) * NA_TQ], e)
        start = geometry(u)[2]
        return _dot(kn_ref[pl.ds(start, n_win), :], qt), _dot(kc_ref[...], qt)

    def finish(u, e, s, s_ctx):
        r, kr0, start = geometry(u)
        win_lo = jnp.clip(r + lane_row - NA_KH // 2, 0, rows - NA_KH)
        tiles = []
        for j in range(NA_KROWS):
            kr = kr0 + j
            in_win = (kr >= win_lo) & (kr < win_lo + NA_KH)
            slabs = [bias_ref[e, jnp.clip(kr - (r + 2 * cp) + NA_KH - 1, 0, NA_SLABS - 1)]
                     for cp in range(NA_QROWS // 2)]
            bias = jnp.concatenate(slabs, axis=1) + jnp.where(in_win, 0.0, NEG)
            tiles.append(s[j * GRID_W:(j + 1) * GRID_W, :] + bias)
        vs = slice(e * NA_HEAD_DIM, (e + 1) * NA_HEAD_DIM)
        return _softmax_pv([jnp.concatenate(tiles, axis=0), s_ctx],
                           [vnt_ref[vs, pl.ds(start, n_win)], vct_ref[vs, :]])

    order = [(u, e) for u in range(n_sub) for e in range(2)]
    outs = {}
    pending = raw_scores(*order[0])
    for i, (u, e) in enumerate(order):
        current = pending
        if i + 1 < len(order):
            pending = raw_scores(*order[i + 1])
        outs[u, e] = finish(u, e, *current)
    o_t = jnp.concatenate([jnp.concatenate([outs[u, e] for u in range(n_sub)], axis=1) for e in range(2)], axis=0)
    o_ref[...] = o_t.T.astype(BF16)


def _na_call(qnt, kn, vnt, knc, vnct, bias, *, rows):
    n = qnt.shape[1]
    lc = knc.shape[0]
    tq = NA_GROUPS_PER_STEP * NA_TQ
    pairs = NA_HEADS // 2
    return pl.pallas_call(
        functools.partial(_na_kernel, rows=rows),
        grid=(pairs, n // tq),
        in_specs=[pl.BlockSpec((LANES, tq), lambda hp, b: (hp, b)),
                  pl.BlockSpec((n, LANES), lambda hp, b: (0, hp)),
                  pl.BlockSpec((LANES, n), lambda hp, b: (hp, 0)),
                  pl.BlockSpec((lc, LANES), lambda hp, b: (0, hp)),
                  pl.BlockSpec((LANES, lc), lambda hp, b: (hp, 0)),
                  pl.BlockSpec((2, NA_SLABS, GRID_W, 2 * GRID_W), lambda hp, b: (hp, 0, 0, 0))],
        out_specs=pl.BlockSpec((tq, LANES), lambda hp, b: (b, hp)),
        out_shape=jax.ShapeDtypeStruct((n, NA_WIDTH), BF16),
        compiler_params=_params(2, VMEM_LIMIT),
        name="na_window",
    )(qnt, kn, vnt, knc, vnct, bias)


def _pair_attn_kernel(qt_ref, k_ref, vt_ref, o_ref):
    outs = []
    for e in range(2):
        vs = slice(e * NA_HEAD_DIM, (e + 1) * NA_HEAD_DIM)
        outs.append(_softmax_pv([_dot(k_ref[...], _head_rows(qt_ref[...], e))], [vt_ref[vs, :]]))
    o_ref[...] = jnp.concatenate(outs, axis=0).T.astype(BF16)


def _pair_attn_call(qt, k, vt):
    n = qt.shape[1]
    fm = lambda cols: pl.BlockSpec((LANES, cols), lambda hp: (hp, 0))
    return pl.pallas_call(
        _pair_attn_kernel,
        grid=(NA_HEADS // 2,),
        in_specs=[fm(n), pl.BlockSpec((k.shape[0], LANES), lambda hp: (0, hp)), fm(vt.shape[1])],
        out_specs=pl.BlockSpec((n, LANES), lambda hp: (0, hp)),
        out_shape=jax.ShapeDtypeStruct((n, NA_WIDTH), BF16),
        compiler_params=_params(1),
        name="na_ctx",
    )(qt, k, vt)


def _na_bias_slabs(rpb):
    w, kw = GRID_W, NA_KW
    c = np.arange(w)[None, :]
    kc = np.arange(w)[:, None]
    c0 = np.clip(c - kw // 2, 0, w - kw)
    col_valid = (kc >= c0) & (kc < c0 + kw)
    col_idx = np.clip(kc - c + (kw - 1), 0, 2 * kw - 2)
    h, n_off, n_col = rpb.shape
    onehot = (col_idx.reshape(1, -1) == np.arange(n_col)[:, None]).astype(np.float32)
    by_off = jnp.dot(rpb.reshape(h * n_off, n_col) * LOG2E, onehot, precision=lax.Precision.HIGHEST)
    by_off = jnp.where(col_valid[None, None], by_off.reshape(h, n_off, w, w), NEG)
    fill = jnp.full((h, NA_SLABS - n_off, w, w), NEG, F32)
    by_off = jnp.concatenate([by_off, fill], axis=1)
    prev = jnp.concatenate([jnp.full((h, 1, w, w), NEG, F32), by_off[:, :-1]], axis=1)
    return jnp.concatenate([by_off, prev], axis=-1)


def _att_sgu_kernel(x_ref, mla_ref, na_ref, sg_ref, amod_ref, awout_ref,
                    mod_ref, ng_ref, win_ref, lng_ref, lnb_ref, ws_ref, bs_ref, wout_ref, fn_ref,
                    o_ref, *, row, final):
    tm, d = x_ref.shape
    a = (mla_ref[...].astype(F32) * sg_ref[:, 0:MLA_WIDTH].astype(F32)).astype(BF16)
    b = (na_ref[...].astype(F32) * sg_ref[:, MLA_WIDTH:].astype(F32)).astype(BF16)
    y_att = _dot(a, awout_ref[0:MLA_WIDTH, :]) + _dot(b, awout_ref[MLA_WIDTH:, :])
    x = x_ref[...] + amod_ref[row:row + 1, 2 * d:3 * d] * y_att

    shift = mod_ref[row:row + 1, 0:d]
    scale = mod_ref[row:row + 1, d:2 * d]
    gate = mod_ref[row:row + 1, 2 * d:3 * d]
    hb = (_rms(x, ng_ref[...]) * (1.0 + scale) + shift).astype(BF16)

    v = _gelu(_dot(hb, win_ref[:, SGU_WIDTH:2 * SGU_WIDTH]))
    mu = jnp.mean(v, axis=-1, keepdims=True)
    vc = v - mu
    var = jnp.mean(vc * vc, axis=-1, keepdims=True)
    vn = (vc * lax.rsqrt(var + EPS) * lng_ref[...] + lnb_ref[...]).astype(BF16)

    y = jnp.zeros((tm, d), F32)
    for g in range(SGU_GROUPS):
        cols = slice(g * SGU_GROUP_DIM, (g + 1) * SGU_GROUP_DIM)
        u = _gelu(_dot(hb, win_ref[:, g * SGU_GROUP_DIM:(g + 1) * SGU_GROUP_DIM]))
        gt = _silu(_dot(hb, win_ref[:, 2 * SGU_WIDTH + g * SGU_GROUP_DIM:
                                    2 * SGU_WIDTH + (g + 1) * SGU_GROUP_DIM]))
        sv = jnp.concatenate(
            [_dot(ws_ref[g], vn[c * SGU_CHUNK:(c + 1) * SGU_CHUNK, cols]) + bs_ref[g]
             for c in range(tm // SGU_CHUNK)], axis=0)
        y = y + _dot((u * sv * gt).astype(BF16), wout_ref[cols, :])
    out = x + gate * y
    if final:
        out = _rms(out, fn_ref[...])
    o_ref[...] = out


def _att_sgu_call(x, att, mod, ng, w, final_norm, *, row, tm, final):
    n, d = x.shape
    mla, na, sg, att_mod, att_w_out = att
    tok = lambda width: pl.BlockSpec((tm, width), lambda i: (i, 0))
    consts = [mod, ng, w["w_in"], w["ln_g"], w["ln_b"], w["w_s"], w["b_s"], w["w_out"], final_norm]
    return pl.pallas_call(
        functools.partial(_att_sgu_kernel, row=row, final=final),
        grid=(n // tm,),
        in_specs=[tok(d), tok(MLA_WIDTH), tok(NA_WIDTH), tok(MLA_WIDTH + NA_WIDTH),
                  _const_spec(att_mod.shape), _const_spec(att_w_out.shape)]
        + [_const_spec(a.shape) for a in consts],
        out_specs=tok(d),
        out_shape=jax.ShapeDtypeStruct((n, d), F32),
        compiler_params=_params(1, VMEM_LIMIT),
        name="att_sgu_ctx" if row else "att_sgu",
    )(x, mla, na, sg, att_mod, att_w_out, *consts)


def _pack_att_weights(w_in, q_norm, w_uq, kv_norm, w_ukv, w_out):
    d = w_in.shape[0]
    bounds = np.cumsum([0, MLA_Q_LORA, MLA_KV_LORA, MLA_ROPE, MLA_WIDTH, NA_WIDTH, NA_WIDTH, NA_WIDTH, NA_WIDTH])
    cq, ckv, kr, gm, qn, kn, vn, gn = [w_in[:, a:b] for a, b in zip(bounds[:-1], bounds[1:])]
    half = MLA_ROPE // 2
    x1, x2 = kr[:, :half], kr[:, half:]
    kr_group = jnp.concatenate([jnp.zeros((d, MLA_NOPE), F32), x1, x2, x2, x1], axis=1)
    na_scale = NA_HEAD_DIM ** -0.5
    packed = jnp.concatenate([cq, ckv, kr_group, gm, qn * na_scale, kn, vn, gn], axis=1)
    assert packed.shape[1] == _C_END

    uq = w_uq.reshape(MLA_Q_LORA, MLA_HEADS, MLA_NOPE + MLA_ROPE)
    q1, q2 = uq[..., MLA_NOPE:MLA_NOPE + half], uq[..., MLA_NOPE + half:]
    uq = jnp.concatenate([uq[..., :MLA_NOPE], q1, q2, q2, q1], axis=-1).reshape(MLA_Q_LORA, MLA_HEADS * LANES)

    ukv = w_ukv.reshape(MLA_KV_LORA, MLA_HEADS, MLA_NOPE + 64)
    wk = jnp.concatenate([ukv[..., :MLA_NOPE], jnp.zeros_like(ukv[..., :LANES - MLA_NOPE])], axis=-1)
    wk = wk.reshape(MLA_KV_LORA, MLA_HEADS * LANES)
    wv = ukv[..., MLA_NOPE:].reshape(MLA_KV_LORA, MLA_WIDTH)
    return {"w_in": packed.astype(BF16), "q_norm": q_norm.reshape(1, -1), "w_uq": uq.astype(BF16),
            "kv_norm": kv_norm.reshape(1, -1), "w_k": wk.astype(BF16), "w_v": wv.astype(BF16),
            "w_out": w_out.astype(BF16)}


def _rope_tables(s, lc):
    rows = s // GRID_W
    axis_dims = MLA_ROPE // 2
    inv = jnp.power(ROPE_BASE, -jnp.arange(0, axis_dims, 2, dtype=F32) / axis_dims)
    n_freq = inv.shape[0]
    ang_row = jnp.arange(rows).astype(F32)[:, None] * inv
    ang_col = jnp.arange(GRID_W).astype(F32)[:, None] * inv

    def per_token(f):
        by_row = jnp.broadcast_to(f(ang_row)[:, None, :], (rows, GRID_W, n_freq))
        by_col = jnp.broadcast_to(f(ang_col)[None, :, :], (rows, GRID_W, n_freq))
        return jnp.concatenate([by_row, by_col], axis=-1).reshape(s, 2 * n_freq)

    cos, sin = per_token(jnp.cos), per_token(jnp.sin)
    z = lambda n, w: jnp.zeros((n, w), F32)
    pad = LANES - MLA_NOPE - MLA_ROPE
    cos_k = jnp.concatenate([z(s, MLA_NOPE), cos, cos, z(s, pad)], axis=1)
    sin_k = jnp.concatenate([z(s, MLA_NOPE), -sin, sin, z(s, pad)], axis=1)
    ctx_k = jnp.concatenate([z(lc, MLA_NOPE), jnp.ones((lc, MLA_ROPE), F32), z(lc, pad)], axis=1)
    return (cos_k, sin_k), (ctx_k, z(lc, LANES))


def kernel(x, c, ctx, c_ctx, norm_g, w_mod, b_mod, att_w_in, mla_q_norm, mla_w_uq, mla_kv_norm, mla_w_ukv,
           na_rpb, att_w_out, sgu_w_in, sgu_ln_g, sgu_ln_b, sgu_w_s, sgu_b_s, sgu_w_out, final_norm):
    batch, s, d = x.shape
    lc = ctx.shape[1]
    depth = norm_g.shape[0]
    assert batch == 1 and c.shape[0] == 1
    rows = s // GRID_W
    assert s % (NA_GROUPS_PER_STEP * NA_TQ) == 0 and rows >= NA_KROWS

    tm = 512 if s % 512 == 0 else 256
    tq = 1024 if s % 1024 == 0 else 256
    tk = 512

    cond = jnp.concatenate([c, c_ctx[None, :], jnp.zeros((6, d), F32)], axis=0)
    mods = _mod_call(cond, w_mod, b_mod)
    tabs_x, tabs_c = _rope_tables(s, lc)
    fnorm = final_norm.reshape(1, d)

    xs, xc = x[0], ctx[0]
    last_ctx_reader = max(l for l in range(depth) if l % 2 == 0)
    att_x = att_c = None
    for l in range(depth):
        i = l // 2
        update_ctx = l < last_ctx_reader
        ng = norm_g[l].reshape(1, d)
        if l % 2 == 0:
            if l == depth - 1:
                raise NotImplementedError("the block must end with a spatial gating layer")
            w = _pack_att_weights(att_w_in[i], mla_q_norm[i], mla_w_uq[i], mla_kv_norm[i], mla_w_ukv[i],
                                  att_w_out[i])
            bias = _na_bias_slabs(na_rpb[i])
            qx, kx, vx, qnx, knx, vnx, sgx = _att_in_call(xs, mods[l], ng, w, tabs_x, row=0, tm=tm)
            qc, kc, vc, qnc, knc, vnc, sgc = _att_in_call(xc, mods[l], ng, w, tabs_c, row=1, tm=lc)
            mla_x = _mla_call(qx, kx, vx, kc, vc, tq=tq, tk=tk)
            na_x = _na_call(qnx, knx, vnx, knc, vnc, bias, rows=rows)
            att_x = (mla_x, na_x, sgx, mods[l], w["w_out"])
            if update_ctx:
                mla_c = _mla_call(qc, None, None, kc, vc, tq=lc, tk=tk)
                na_c = _pair_attn_call(qnc, knc, vnc)
                att_c = (mla_c, na_c, sgc, mods[l], w["w_out"])
        else:
            w = {"w_in": sgu_w_in[i].astype(BF16), "ln_g": sgu_ln_g[i].reshape(1, -1),
                 "ln_b": sgu_ln_b[i].reshape(1, -1), "w_s": sgu_w_s[i].astype(BF16),
                 "b_s": sgu_b_s[i][:, :, None], "w_out": sgu_w_out[i].astype(BF16)}
            final = l == depth - 1
            xs = _att_sgu_call(xs, att_x, mods[l], ng, w, fnorm, row=0, tm=tm, final=final)
            if update_ctx:
                xc = _att_sgu_call(xc, att_c, mods[l], ng, w, fnorm, row=1, tm=lc, final=False)
    return xs[None]
```

```python
import functools
import math

import numpy as np
import jax
import jax.numpy as jnp
from jax import lax
from jax.experimental import pallas as pl
from jax.experimental.pallas import tpu as pltpu

F32 = jnp.float32
BF16 = jnp.bfloat16

EPS = 1e-6
NEG = -1e30
LOG2E = math.log2(math.e)
LANES = 128
VMEM_LIMIT = 56 * 1024 * 1024

GRID_W = 64
MLA_HEADS = 8
MLA_NOPE = 64
MLA_ROPE = 32
MLA_V = 64
SUM_ROWS = 16
MLA_UNROLL = 8
MLA_Q_LORA = 768
MLA_KV_LORA = 256
MLA_WIDTH = 512
ROPE_BASE = 10000.0
NA_HEADS = 8
NA_HEAD_DIM = 64
NA_WIDTH = 512
NA_KH = 8
NA_KW = 16
SGU_CHUNK = 128
SGU_WIDTH = 2048
SGU_GROUPS = 8
SGU_GROUP_DIM = SGU_WIDTH // SGU_GROUPS

NA_QROWS = 4
NA_KROWS = 12
NA_TQ = NA_QROWS * GRID_W
NA_GROUPS_PER_STEP = 4
NA_SLABS = 16


def _dot(a, b):
    return jnp.dot(a, b, preferred_element_type=F32)


def _rms(x, g):
    return x * lax.rsqrt(jnp.mean(x * x, axis=-1, keepdims=True) + EPS) * g


def _silu(x):
    return x * jax.nn.sigmoid(x)


def _gelu(x):
    return 0.5 * x * (1.0 + lax.erf(x * np.float32(math.sqrt(0.5))))


def _params(n_axes, vmem=None):
    return pltpu.CompilerParams(dimension_semantics=("arbitrary",) * n_axes,
                                vmem_limit_bytes=vmem)


def _const_spec(shape):
    zeros = (0,) * len(shape)
    return pl.BlockSpec(shape, lambda *_: zeros, pipeline_mode=pl.Buffered(1))


def _mod_kernel(cond_ref, w_ref, b_ref, o_ref):
    s = _silu(cond_ref[...])
    w = w_ref[0]
    s_hi = s.astype(BF16)
    s_lo = (s - s_hi.astype(F32)).astype(BF16)
    w_hi = w.astype(BF16)
    w_lo = (w - w_hi.astype(F32)).astype(BF16)
    o_ref[0] = _dot(s_hi, w_hi) + _dot(s_hi, w_lo) + _dot(s_lo, w_hi) + b_ref[0]


def _mod_call(cond, w_mod, b_mod):
    depth, d, d3 = w_mod.shape
    tn = 1024
    return pl.pallas_call(
        _mod_kernel,
        grid=(depth, d3 // tn),
        in_specs=[pl.BlockSpec((8, d), lambda l, j: (0, 0)),
                  pl.BlockSpec((1, d, tn), lambda l, j: (l, 0, j)),
                  pl.BlockSpec((1, 1, tn), lambda l, j: (l, 0, j))],
        out_specs=pl.BlockSpec((1, 8, tn), lambda l, j: (l, 0, j)),
        out_shape=jax.ShapeDtypeStruct((depth, 8, d3), F32),
        compiler_params=_params(2),
        name="adaln_mod",
    )(cond, w_mod, b_mod.reshape(depth, 1, d3))


_C_CQ = 0
_C_CKV = _C_CQ + MLA_Q_LORA
_C_KR = _C_CKV + MLA_KV_LORA
_C_GM = _C_KR + LANES
_C_QN = _C_GM + MLA_WIDTH
_C_KN = _C_QN + NA_WIDTH
_C_VN = _C_KN + NA_WIDTH
_C_GN = _C_VN + NA_WIDTH
_C_END = _C_GN + NA_WIDTH


def _att_in_kernel(x_ref, mod_ref, ng_ref, win_ref, qnorm_ref, wuq_ref, kvnorm_ref, wk_ref, wv_ref,
                   ck_ref, sk_ref,
                   qt_ref, k_ref, vt_ref, qnt_ref, kn_ref, vnt_ref, sg_ref, *, row):
    d = x_ref.shape[1]
    shift = mod_ref[row:row + 1, 0:d]
    scale = mod_ref[row:row + 1, d:2 * d]
    hb = (_rms(x_ref[...], ng_ref[...]) * (1.0 + scale) + shift).astype(BF16)

    def proj(lo, hi):
        return _dot(hb, win_ref[:, lo:hi])

    cqn = _rms(proj(_C_CQ, _C_CKV), qnorm_ref[...]).astype(BF16)
    q = _dot(cqn, wuq_ref[...])
    q_sw = pltpu.roll(q, q.shape[1] - MLA_ROPE, 1)
    ck_t, sk_t = ck_ref[...], sk_ref[...]
    nope = (lax.broadcasted_iota(jnp.int32, (1, LANES), 1) < MLA_NOPE).astype(F32)
    q_scale = (MLA_NOPE + MLA_ROPE) ** -0.5 * LOG2E
    cq_t, sq_t = (ck_t + nope) * q_scale, sk_t * q_scale
    for h in range(MLA_HEADS):
        sl = slice(h * LANES, (h + 1) * LANES)
        qt_ref[sl, :] = (q[:, sl] * cq_t + q_sw[:, sl] * sq_t).T.astype(BF16)

    ckvn = _rms(proj(_C_CKV, _C_KR), kvnorm_ref[...]).astype(BF16)
    k_nope = _dot(ckvn, wk_ref[...])
    kr = proj(_C_KR, _C_GM)
    kr_rot = kr * ck_t + pltpu.roll(kr, LANES - MLA_ROPE, 1) * sk_t
    for h in range(MLA_HEADS):
        sl = slice(h * LANES, (h + 1) * LANES)
        k_ref[:, sl] = (k_nope[:, sl] + kr_rot).astype(BF16)
    vt_ref[...] = _dot(ckvn, wv_ref[...]).T.astype(BF16)

    qnt_ref[...] = (proj(_C_QN, _C_KN) * LOG2E).T.astype(BF16)
    kn_ref[...] = proj(_C_KN, _C_VN).astype(BF16)
    vnt_ref[...] = proj(_C_VN, _C_GN).T.astype(BF16)
    sg_ref[:, 0:MLA_WIDTH] = _silu(proj(_C_GM, _C_QN)).astype(BF16)
    sg_ref[:, MLA_WIDTH:] = _silu(proj(_C_GN, _C_END)).astype(BF16)


def _att_in_call(x, mod, ng, w, tabs, *, row, tm):
    n, d = x.shape
    hw = MLA_HEADS * LANES
    tok = lambda width: pl.BlockSpec((tm, width), lambda i: (i, 0))
    tok_t = lambda width: pl.BlockSpec((width, tm), lambda i: (0, i))
    out_widths = (hw, hw, MLA_WIDTH, NA_WIDTH, NA_WIDTH, NA_WIDTH, MLA_WIDTH + NA_WIDTH)
    transposed = (True, False, True, True, False, True, False)
    return pl.pallas_call(
        functools.partial(_att_in_kernel, row=row),
        grid=(n // tm,),
        in_specs=[tok(d), _const_spec(mod.shape), _const_spec(ng.shape),
                  _const_spec(w["w_in"].shape), _const_spec(w["q_norm"].shape),
                  _const_spec(w["w_uq"].shape), _const_spec(w["kv_norm"].shape),
                  _const_spec(w["w_k"].shape), _const_spec(w["w_v"].shape),
                  tok(LANES), tok(LANES)],
        out_specs=[tok_t(wd) if t else tok(wd) for wd, t in zip(out_widths, transposed)],
        out_shape=[jax.ShapeDtypeStruct((wd, n) if t else (n, wd), BF16)
                   for wd, t in zip(out_widths, transposed)],
        compiler_params=_params(1, VMEM_LIMIT),
        name="att_in_ctx" if row else "att_in",
    )(x, mod, ng, w["w_in"], w["q_norm"], w["w_uq"], w["kv_norm"], w["w_k"], w["w_v"], *tabs)


def _mla_kernel(*refs, tk, n_chunks):
    heads = (0, 1)
    if n_chunks:
        qt_ref, kx_ref, vxt_ref, kc_ref, vct_ref, o_ref = refs[:6]
        acc_refs, s_refs = refs[6:8], refs[8:12]
    else:
        qt_ref, kc_ref, vct_ref, o_ref = refs[:4]
        acc_refs = refs[4:6]
    tq = qt_ref.shape[1]
    hs = [slice(e * LANES, (e + 1) * LANES) for e in heads]
    vs = [slice(e * MLA_V, (e + 1) * MLA_V) for e in heads]

    def chunk(j):
        return pl.ds(j * tk if isinstance(j, int) else pl.multiple_of(j * tk, tk), tk)

    def scores(e, j):
        return _dot(kx_ref[chunk(j), hs[e]], qt_ref[hs[e], :])

    def ctx_scores(e):
        return _dot(kc_ref[:, hs[e]], qt_ref[hs[e], :])

    def values(e, j):
        return vxt_ref[vs[e], chunk(j)]

    def col_max(s):
        return jnp.max(s, axis=0, keepdims=True)

    def update(e, s, s_max, vt, m):
        m_new = jnp.maximum(m, s_max)
        alpha = jnp.exp2(m - m_new)
        p = jnp.exp2(s - m_new).astype(BF16)
        ones_row = (lax.broadcasted_iota(jnp.int32, (SUM_ROWS, vt.shape[1]), 0) == 0).astype(BF16)
        acc = acc_refs[e]
        acc[...] = alpha * acc[...] + _dot(jnp.concatenate([vt, ones_row], axis=0), p)
        return m_new

    for e in heads:
        acc_refs[e][...] = jnp.zeros(acc_refs[e].shape, F32)
    ms = [jnp.full((1, tq), NEG, F32) for _ in heads]
    if n_chunks:
        s_ctx = [None, None]

        def issue(e, buf, c):
            s = scores(e, c)
            s_refs[2 * e + buf][...] = s
            return col_max(s)

        def step(c, parity, ms, maxes, last=False):
            ms, maxes = list(ms), list(maxes)
            for e in heads:
                cur_max = maxes[e]
                if last:
                    s_ctx[e] = ctx_scores(e)
                else:
                    maxes[e] = issue(e, 1 - parity, c + 1)
                ms[e] = update(e, s_refs[2 * e + parity][...], cur_max, values(e, c), ms[e])
            return ms, maxes

        maxes = [issue(e, 0, 0) for e in heads]
        trips = (n_chunks - 1) // MLA_UNROLL

        def body(t, carry):
            ms, maxes = carry[:2], carry[2:]
            for i in range(MLA_UNROLL):
                ms, maxes = step(MLA_UNROLL * t + i, i % 2, ms, maxes)
            return tuple(ms) + tuple(maxes)
        carry = lax.fori_loop(0, trips, body, tuple(ms) + tuple(maxes))
        ms, maxes = list(carry[:2]), list(carry[2:])
        for c in range(trips * MLA_UNROLL, n_chunks):
            ms, maxes = step(c, c % 2, ms, maxes, last=c == n_chunks - 1)
    else:
        s_ctx = [ctx_scores(e) for e in heads]
    for e in heads:
        update(e, s_ctx[e], col_max(s_ctx[e]), vct_ref[vs[e], :], ms[e])
    outs = [acc_refs[e][0:MLA_V, :] / acc_refs[e][MLA_V:MLA_V + 1, :] for e in heads]
    o_ref[...] = jnp.concatenate(outs, axis=0).T.astype(BF16)


def _mla_call(qt, kx, vxt, kc, vct, *, tq, tk):
    n = qt.shape[1]
    lc = kc.shape[0]
    pairs = MLA_HEADS // 2
    in_specs = [pl.BlockSpec((2 * LANES, tq), lambda hp, i: (hp, i))]
    args = [qt]
    n_chunks = 0
    if kx is not None:
        s = kx.shape[0]
        n_chunks = s // tk
        assert s % tk == 0 and n_chunks % 2 == 0
        in_specs += [pl.BlockSpec((s, 2 * LANES), lambda hp, i: (0, hp)),
                     pl.BlockSpec((LANES, s), lambda hp, i: (hp, 0))]
        args += [kx, vxt]
    in_specs += [pl.BlockSpec((lc, 2 * LANES), lambda hp, i: (0, hp)),
                 pl.BlockSpec((LANES, lc), lambda hp, i: (hp, 0))]
    args += [kc, vct]
    return pl.pallas_call(
        functools.partial(_mla_kernel, tk=tk, n_chunks=n_chunks),
        grid=(pairs, n // tq),
        in_specs=in_specs,
        out_specs=pl.BlockSpec((tq, LANES), lambda hp, i: (i, hp)),
        out_shape=jax.ShapeDtypeStruct((n, MLA_WIDTH), BF16),
        scratch_shapes=[pltpu.VMEM((MLA_V + SUM_ROWS, tq), F32)] * 2
        + [pltpu.VMEM((tk, tq), F32)] * (4 if n_chunks else 0),
        compiler_params=_params(2, VMEM_LIMIT),
        name="mla_flash" if kx is not None else "mla_ctx",
    )(*args)


def _head_rows(qt, e):
    row = lax.broadcasted_iota(jnp.int32, qt.shape, 0)
    keep = (row >= e * NA_HEAD_DIM) & (row < (e + 1) * NA_HEAD_DIM)
    return jnp.where(keep, qt, jnp.zeros_like(qt))


def _with_ones_row(vt):
    ones_row = (lax.broadcasted_iota(jnp.int32, (SUM_ROWS, vt.shape[1]), 0) == 0).astype(BF16)
    return jnp.concatenate([vt, ones_row], axis=0)


def _softmax_pv(s_list, vt_list, m=None):
    if m is None:
        m = s_list[0].max(axis=0, keepdims=True)
        for s in s_list[1:]:
            m = jnp.maximum(m, s.max(axis=0, keepdims=True))
    acc = None
    for s, vt in zip(s_list, vt_list):
        part = _dot(_with_ones_row(vt), jnp.exp2(s - m).astype(BF16))
        acc = part if acc is None else acc + part
    return acc[0:NA_HEAD_DIM, :] / acc[NA_HEAD_DIM:NA_HEAD_DIM + 1, :]


def _na_kernel(qt_ref, kn_ref, vnt_ref, kc_ref, vct_ref, bias_ref, o_ref, *, rows):
    n_sub = qt_ref.shape[1] // NA_TQ
    blk = pl.program_id(1)
    lane_row = lax.broadcasted_iota(jnp.int32, (1, NA_TQ), 1) // GRID_W
    n_win = NA_KROWS * GRID_W

    def geometry(u):
        r = (blk * n_sub + u) * NA_QROWS
        kr0 = jnp.clip(r - NA_KH // 2, 0, rows - NA_KROWS)
        return r, kr0, pl.multiple_of(kr0 * GRID_W, 2 * GRID_W)

    def biased_scores(u, e):
        r, kr0, start = geometry(u)
        qt = _head_rows(qt_ref[:, u * NA_TQ:(u + 1) * NA_TQ], e)
        s = _dot(kn_ref[pl.ds(start, n_win), :], qt)
        s_ctx = _dot(kc_ref[...], qt)
        win_lo = jnp.clip(r + lane_row - NA_KH // 2, 0, rows - NA_KH)
        tiles = []
        for j in range(NA_KROWS):
            kr = kr0 + j
            in_win = (kr >= win_lo) & (kr < win_lo + NA_KH)
            slabs = [bias_ref[e, jnp.clip(kr - (r + 2 * cp) + NA_KH - 1, 0, NA_SLABS - 1)]
                     for cp in range(NA_QROWS // 2)]
            bias = jnp.concatenate(slabs, axis=1) + jnp.where(in_win, 0.0, NEG)
            tiles.append(s[j * GRID_W:(j + 1) * GRID_W, :] + bias)
        s = jnp.concatenate(tiles, axis=0)
        return s, s_ctx, jnp.maximum(s.max(axis=0, keepdims=True), s_ctx.max(axis=0, keepdims=True))

    def finish(u, e, s, s_ctx, m):
        start = geometry(u)[2]
        vs = slice(e * NA_HEAD_DIM, (e + 1) * NA_HEAD_DIM)
        return _softmax_pv([s, s_ctx], [vnt_ref[vs, pl.ds(start, n_win)], vct_ref[vs, :]], m)

    order = [(u, e) for u in range(n_sub) for e in range(2)]
    outs = {}
    pending = biased_scores(*order[0])
    for i, (u, e) in enumerate(order):
        current = pending
        if i + 1 < len(order):
            pending = biased_scores(*order[i + 1])
        outs[u, e] = finish(u, e, *current)
    o_t = jnp.concatenate([jnp.concatenate([outs[u, e] for u in range(n_sub)], axis=1) for e in range(2)], axis=0)
    o_ref[...] = o_t.T.astype(BF16)


def _na_call(qnt, kn, vnt, knc, vnct, bias, *, rows):
    n = qnt.shape[1]
    lc = knc.shape[0]
    tq = NA_GROUPS_PER_STEP * NA_TQ
    pairs = NA_HEADS // 2
    return pl.pallas_call(
        functools.partial(_na_kernel, rows=rows),
        grid=(pairs, n // tq),
        in_specs=[pl.BlockSpec((LANES, tq), lambda hp, b: (hp, b)),
                  pl.BlockSpec((n, LANES), lambda hp, b: (0, hp)),
                  pl.BlockSpec((LANES, n), lambda hp, b: (hp, 0)),
                  pl.BlockSpec((lc, LANES), lambda hp, b: (0, hp)),
                  pl.BlockSpec((LANES, lc), lambda hp, b: (hp, 0)),
                  pl.BlockSpec((2, NA_SLABS, GRID_W, 2 * GRID_W), lambda hp, b: (hp, 0, 0, 0))],
        out_specs=pl.BlockSpec((tq, LANES), lambda hp, b: (b, hp)),
        out_shape=jax.ShapeDtypeStruct((n, NA_WIDTH), BF16),
        compiler_params=_params(2, VMEM_LIMIT),
        name="na_window",
    )(qnt, kn, vnt, knc, vnct, bias)


def _pair_attn_kernel(qt_ref, k_ref, vt_ref, o_ref):
    outs = []
    for e in range(2):
        vs = slice(e * NA_HEAD_DIM, (e + 1) * NA_HEAD_DIM)
        outs.append(_softmax_pv([_dot(k_ref[...], _head_rows(qt_ref[...], e))], [vt_ref[vs, :]]))
    o_ref[...] = jnp.concatenate(outs, axis=0).T.astype(BF16)


def _pair_attn_call(qt, k, vt):
    n = qt.shape[1]
    fm = lambda cols: pl.BlockSpec((LANES, cols), lambda hp: (hp, 0))
    return pl.pallas_call(
        _pair_attn_kernel,
        grid=(NA_HEADS // 2,),
        in_specs=[fm(n), pl.BlockSpec((k.shape[0], LANES), lambda hp: (0, hp)), fm(vt.shape[1])],
        out_specs=pl.BlockSpec((n, LANES), lambda hp: (0, hp)),
        out_shape=jax.ShapeDtypeStruct((n, NA_WIDTH), BF16),
        compiler_params=_params(1),
        name="na_ctx",
    )(qt, k, vt)


def _na_bias_slabs(rpb):
    w, kw = GRID_W, NA_KW
    c = np.arange(w)[None, :]
    kc = np.arange(w)[:, None]
    c0 = np.clip(c - kw // 2, 0, w - kw)
    col_valid = (kc >= c0) & (kc < c0 + kw)
    col_idx = np.clip(kc - c + (kw - 1), 0, 2 * kw - 2)
    h, n_off, n_col = rpb.shape
    onehot = (col_idx.reshape(1, -1) == np.arange(n_col)[:, None]).astype(np.float32)
    by_off = jnp.dot(rpb.reshape(h * n_off, n_col) * LOG2E, onehot, precision=lax.Precision.HIGHEST)
    by_off = jnp.where(col_valid[None, None], by_off.reshape(h, n_off, w, w), NEG)
    fill = jnp.full((h, NA_SLABS - n_off, w, w), NEG, F32)
    by_off = jnp.concatenate([by_off, fill], axis=1)
    prev = jnp.concatenate([jnp.full((h, 1, w, w), NEG, F32), by_off[:, :-1]], axis=1)
    return jnp.concatenate([by_off, prev], axis=-1)


def _att_sgu_kernel(x_ref, mla_ref, na_ref, sg_ref, amod_ref, awout_ref,
                    mod_ref, ng_ref, win_ref, lng_ref, lnb_ref, ws_ref, bs_ref, wout_ref, fn_ref,
                    o_ref, *, row, final):
    tm, d = x_ref.shape
    a = (mla_ref[...].astype(F32) * sg_ref[:, 0:MLA_WIDTH].astype(F32)).astype(BF16)
    b = (na_ref[...].astype(F32) * sg_ref[:, MLA_WIDTH:].astype(F32)).astype(BF16)
    y_att = _dot(a, awout_ref[0:MLA_WIDTH, :]) + _dot(b, awout_ref[MLA_WIDTH:, :])
    x = x_ref[...] + amod_ref[row:row + 1, 2 * d:3 * d] * y_att

    shift = mod_ref[row:row + 1, 0:d]
    scale = mod_ref[row:row + 1, d:2 * d]
    gate = mod_ref[row:row + 1, 2 * d:3 * d]
    hb = (_rms(x, ng_ref[...]) * (1.0 + scale) + shift).astype(BF16)

    v = _gelu(_dot(hb, win_ref[:, SGU_WIDTH:2 * SGU_WIDTH]))
    mu = jnp.mean(v, axis=-1, keepdims=True)
    vc = v - mu
    var = jnp.mean(vc * vc, axis=-1, keepdims=True)
    vn = (vc * lax.rsqrt(var + EPS) * lng_ref[...] + lnb_ref[...]).astype(BF16)

    y = jnp.zeros((tm, d), F32)
    for g in range(SGU_GROUPS):
        cols = slice(g * SGU_GROUP_DIM, (g + 1) * SGU_GROUP_DIM)
        u = _gelu(_dot(hb, win_ref[:, g * SGU_GROUP_DIM:(g + 1) * SGU_GROUP_DIM]))
        gt = _silu(_dot(hb, win_ref[:, 2 * SGU_WIDTH + g * SGU_GROUP_DIM:
                                    2 * SGU_WIDTH + (g + 1) * SGU_GROUP_DIM]))
        sv = jnp.concatenate(
            [_dot(ws_ref[g], vn[c * SGU_CHUNK:(c + 1) * SGU_CHUNK, cols]) + bs_ref[g]
             for c in range(tm // SGU_CHUNK)], axis=0)
        y = y + _dot((u * sv * gt).astype(BF16), wout_ref[cols, :])
    out = x + gate * y
    if final:
        out = _rms(out, fn_ref[...])
    o_ref[...] = out


def _att_sgu_call(x, att, mod, ng, w, final_norm, *, row, tm, final):
    n, d = x.shape
    mla, na, sg, att_mod, att_w_out = att
    tok = lambda width: pl.BlockSpec((tm, width), lambda i: (i, 0))
    consts = [mod, ng, w["w_in"], w["ln_g"], w["ln_b"], w["w_s"], w["b_s"], w["w_out"], final_norm]
    return pl.pallas_call(
        functools.partial(_att_sgu_kernel, row=row, final=final),
        grid=(n // tm,),
        in_specs=[tok(d), tok(MLA_WIDTH), tok(NA_WIDTH), tok(MLA_WIDTH + NA_WIDTH),
                  _const_spec(att_mod.shape), _const_spec(att_w_out.shape)]
        + [_const_spec(a.shape) for a in consts],
        out_specs=tok(d),
        out_shape=jax.ShapeDtypeStruct((n, d), F32),
        compiler_params=_params(1, VMEM_LIMIT),
        name="att_sgu_ctx" if row else "att_sgu",
    )(x, mla, na, sg, att_mod, att_w_out, *consts)


def _pack_att_weights(w_in, q_norm, w_uq, kv_norm, w_ukv, w_out):
    d = w_in.shape[0]
    bounds = np.cumsum([0, MLA_Q_LORA, MLA_KV_LORA, MLA_ROPE, MLA_WIDTH, NA_WIDTH, NA_WIDTH, NA_WIDTH, NA_WIDTH])
    cq, ckv, kr, gm, qn, kn, vn, gn = [w_in[:, a:b] for a, b in zip(bounds[:-1], bounds[1:])]
    half = MLA_ROPE // 2
    x1, x2 = kr[:, :half], kr[:, half:]
    kr_group = jnp.concatenate([jnp.zeros((d, MLA_NOPE), F32), x1, x2, x2, x1], axis=1)
    na_scale = NA_HEAD_DIM ** -0.5
    packed = jnp.concatenate([cq, ckv, kr_group, gm, qn * na_scale, kn, vn, gn], axis=1)
    assert packed.shape[1] == _C_END

    uq = w_uq.reshape(MLA_Q_LORA, MLA_HEADS, MLA_NOPE + MLA_ROPE)
    q1, q2 = uq[..., MLA_NOPE:MLA_NOPE + half], uq[..., MLA_NOPE + half:]
    uq = jnp.concatenate([uq[..., :MLA_NOPE], q1, q2, q2, q1], axis=-1).reshape(MLA_Q_LORA, MLA_HEADS * LANES)

    ukv = w_ukv.reshape(MLA_KV_LORA, MLA_HEADS, MLA_NOPE + 64)
    wk = jnp.concatenate([ukv[..., :MLA_NOPE], jnp.zeros_like(ukv[..., :LANES - MLA_NOPE])], axis=-1)
    wk = wk.reshape(MLA_KV_LORA, MLA_HEADS * LANES)
    wv = ukv[..., MLA_NOPE:].reshape(MLA_KV_LORA, MLA_WIDTH)
    return {"w_in": packed.astype(BF16), "q_norm": q_norm.reshape(1, -1), "w_uq": uq.astype(BF16),
            "kv_norm": kv_norm.reshape(1, -1), "w_k": wk.astype(BF16), "w_v": wv.astype(BF16),
            "w_out": w_out.astype(BF16)}


def _rope_tables(s, lc):
    rows = s // GRID_W
    axis_dims = MLA_ROPE // 2
    inv = jnp.power(ROPE_BASE, -jnp.arange(0, axis_dims, 2, dtype=F32) / axis_dims)
    n_freq = inv.shape[0]
    ang_row = jnp.arange(rows).astype(F32)[:, None] * inv
    ang_col = jnp.arange(GRID_W).astype(F32)[:, None] * inv

    def per_token(f):
        by_row = jnp.broadcast_to(f(ang_row)[:, None, :], (rows, GRID_W, n_freq))
        by_col = jnp.broadcast_to(f(ang_col)[None, :, :], (rows, GRID_W, n_freq))
        return jnp.concatenate([by_row, by_col], axis=-1).reshape(s, 2 * n_freq)

    cos, sin = per_token(jnp.cos), per_token(jnp.sin)
    z = lambda n, w: jnp.zeros((n, w), F32)
    pad = LANES - MLA_NOPE - MLA_ROPE
    cos_k = jnp.concatenate([z(s, MLA_NOPE), cos, cos, z(s, pad)], axis=1)
    sin_k = jnp.concatenate([z(s, MLA_NOPE), -sin, sin, z(s, pad)], axis=1)
    ctx_k = jnp.concatenate([z(lc, MLA_NOPE), jnp.ones((lc, MLA_ROPE), F32), z(lc, pad)], axis=1)
    return (cos_k, sin_k), (ctx_k, z(lc, LANES))


def kernel(x, c, ctx, c_ctx, norm_g, w_mod, b_mod, att_w_in, mla_q_norm, mla_w_uq, mla_kv_norm, mla_w_ukv,
           na_rpb, att_w_out, sgu_w_in, sgu_ln_g, sgu_ln_b, sgu_w_s, sgu_b_s, sgu_w_out, final_norm):
    batch, s, d = x.shape
    lc = ctx.shape[1]
    depth = norm_g.shape[0]
    assert batch == 1 and c.shape[0] == 1
    rows = s // GRID_W
    assert s % (NA_GROUPS_PER_STEP * NA_TQ) == 0 and rows >= NA_KROWS

    tm = 512 if s % 512 == 0 else 256
    tq = 1024 if s % 1024 == 0 else 256
    tk = 512

    cond = jnp.concatenate([c, c_ctx[None, :], jnp.zeros((6, d), F32)], axis=0)
    mods = _mod_call(cond, w_mod, b_mod)
    tabs_x, tabs_c = _rope_tables(s, lc)
    fnorm = final_norm.reshape(1, d)

    xs, xc = x[0], ctx[0]
    last_ctx_reader = max(l for l in range(depth) if l % 2 == 0)
    att_x = att_c = None
    for l in range(depth):
        i = l // 2
        update_ctx = l < last_ctx_reader
        ng = norm_g[l].reshape(1, d)
        if l % 2 == 0:
            if l == depth - 1:
                raise NotImplementedError("the block must end with a spatial gating layer")
            w = _pack_att_weights(att_w_in[i], mla_q_norm[i], mla_w_uq[i], mla_kv_norm[i], mla_w_ukv[i],
                                  att_w_out[i])
            bias = _na_bias_slabs(na_rpb[i])
            qx, kx, vx, qnx, knx, vnx, sgx = _att_in_call(xs, mods[l], ng, w, tabs_x, row=0, tm=tm)
            qc, kc, vc, qnc, knc, vnc, sgc = _att_in_call(xc, mods[l], ng, w, tabs_c, row=1, tm=lc)
            mla_x = _mla_call(qx, kx, vx, kc, vc, tq=tq, tk=tk)
            na_x = _na_call(qnx, knx, vnx, knc, vnc, bias, rows=rows)
            att_x = (mla_x, na_x, sgx, mods[l], w["w_out"])
            if update_ctx:
                mla_c = _mla_call(qc, None, None, kc, vc, tq=lc, tk=tk)
                na_c = _pair_attn_call(qnc, knc, vnc)
                att_c = (mla_c, na_c, sgc, mods[l], w["w_out"])
        else:
            w = {"w_in": sgu_w_in[i].astype(BF16), "ln_g": sgu_ln_g[i].reshape(1, -1),
                 "ln_b": sgu_ln_b[i].reshape(1, -1), "w_s": sgu_w_s[i].astype(BF16),
                 "b_s": sgu_b_s[i][:, :, None], "w_out": sgu_w_out[i].astype(BF16)}
            final = l == depth - 1
            xs = _att_sgu_call(xs, att_x, mods[l], ng, w, fnorm, row=0, tm=tm, final=final)
            if update_ctx:
                xc = _att_sgu_call(xc, att_c, mods[l], ng, w, fnorm, row=1, tm=lc, final=False)
    return xs[None]
```
